```python
import math
import jax, jax.numpy as jnp
from jax import lax
import numpy as np

D_MODEL = 1024
BATCH = 32
SEQ = 256
DEPTH = 2
DEC_BATCH = 4
DEC_SEQ = 2048
PAST_LEN = 256

GRID_W = 64
HEAD_DIM = 64
ATTN_Q_HEADS = 6
ATTN_KV_HEADS = 2
DIFF_HEADS = 4
DIFF_QK_DIM = 48
DIFF_V_DIM = 96
FOURIER_GROUPS = 4
FOURIER_DIM = 64
MIX_WIDTH = ATTN_Q_HEADS * HEAD_DIM + DIFF_HEADS * DIFF_V_DIM + FOURIER_GROUPS * FOURIER_DIM
_IN_SIZES = (ATTN_Q_HEADS * HEAD_DIM, ATTN_KV_HEADS * HEAD_DIM, ATTN_KV_HEADS * HEAD_DIM,
             DIFF_HEADS * 2 * DIFF_QK_DIM, DIFF_HEADS * 2 * DIFF_QK_DIM, DIFF_HEADS * DIFF_V_DIM,
             FOURIER_GROUPS * FOURIER_DIM)
IN_WIDTH = sum(_IN_SIZES)
N_GROUPS = 4
EXPERTS_PER_GROUP = 8
N_EXPERTS = N_GROUPS * EXPERTS_PER_GROUP
TOP_K = 2
D_EXPERT = 512
Q_BLOCK = 128
ROPE_THETA = 10000.0
EPS = 1e-6
N_MOD = 6

kernel_name = "hybrid_prefix_diffusion_step"


def _rms(x, g):
    xf = x.astype(jnp.float32)
    y = xf * lax.rsqrt(jnp.mean(xf * xf, axis=-1, keepdims=True) + EPS)
    return (y * g.astype(jnp.float32)).astype(x.dtype)


def _axial_angles(T, dim):
    rows = T // GRID_W
    r = jnp.repeat(jnp.arange(rows), GRID_W).astype(jnp.float32)
    col = jnp.tile(jnp.arange(GRID_W), rows).astype(jnp.float32)
    n = dim // 4
    freqs = ROPE_THETA ** (-jnp.arange(n, dtype=jnp.float32) / n)
    return (r[:, None] * freqs, col[:, None] * freqs)


def _rope_1d(x, ang):
    n = ang.shape[-1]
    shape = (1, ang.shape[0]) + (1,) * (x.ndim - 3) + (n,)
    cos = jnp.cos(ang).reshape(shape)
    sin = jnp.sin(ang).reshape(shape)
    xf = x.astype(jnp.float32)
    x1, x2 = xf[..., :n], xf[..., n:]
    return jnp.concatenate([x1 * cos - x2 * sin, x1 * sin + x2 * cos], axis=-1).astype(x.dtype)


def _rope_2d(x, angs):
    half = x.shape[-1] // 2
    return jnp.concatenate([_rope_1d(x[..., :half], angs[0]), _rope_1d(x[..., half:], angs[1])], axis=-1)


def _blocked(fn, q):
    B, T = q.shape[:2]
    nb = T // Q_BLOCK
    qb = jnp.moveaxis(q.reshape((B, nb, Q_BLOCK) + q.shape[2:]), 1, 0)
    o = lax.map(fn, qb)
    return jnp.moveaxis(o, 0, 1).reshape((B, T) + o.shape[3:])


def _gqa(q, k, v):
    B = q.shape[0]
    G = ATTN_Q_HEADS // ATTN_KV_HEADS
    scale = HEAD_DIM ** -0.5

    def blk(qb):
        qg = qb.reshape(B, Q_BLOCK, ATTN_KV_HEADS, G, HEAD_DIM)
        s = jnp.einsum('bqkgd,bskd->bkgqs', qg, k, preferred_element_type=jnp.float32) * scale
        p = jax.nn.softmax(s, axis=-1).astype(v.dtype)
        o = jnp.einsum('bkgqs,bskd->bqkgd', p, v)
        return o.reshape(B, Q_BLOCK, ATTN_Q_HEADS, HEAD_DIM)

    return _blocked(blk, q)


def _diff_attn(q, k, v, lam):
    scale = DIFF_QK_DIM ** -0.5

    def blk(qb):
        s = jnp.einsum('bqhjd,bshjd->bhjqs', qb, k, preferred_element_type=jnp.float32) * scale
        p = jax.nn.softmax(s, axis=-1)
        a = (p[:, :, 0] - lam * p[:, :, 1]).astype(v.dtype)
        return jnp.einsum('bhqs,bshe->bqhe', a, v)

    return _blocked(blk, q)


def _fourier(u):
    B, T = u.shape[:2]
    y = jnp.fft.fft2(u.astype(jnp.float32), axes=(1, 3), norm='ortho').real
    return y.astype(u.dtype).reshape(B, T, FOURIER_GROUPS * FOURIER_DIM)


def _moe(h, w_grp, b_grp, w_exp, b_exp, w_gate, w_up, w_down):
    B, T, Dm = h.shape
    t = h.reshape(-1, Dm)
    N = t.shape[0]
    pg = jax.nn.softmax((t @ w_grp).astype(jnp.float32) + b_grp.astype(jnp.float32), axis=-1)
    gi = jnp.argmax(pg, axis=-1)
    pg_top = jnp.max(pg, axis=-1)
    le = ((t @ w_exp).astype(jnp.float32) + b_exp.astype(jnp.float32)).reshape(N, N_GROUPS, EXPERTS_PER_GROUP)
    le_sel = jnp.take_along_axis(le, gi[:, None, None], axis=1)[:, 0]
    pe = jax.nn.softmax(le_sel, axis=-1)
    vals, idx = lax.top_k(pe, TOP_K)
    vals = vals / jnp.sum(vals, axis=-1, keepdims=True)
    eid = gi[:, None] * EXPERTS_PER_GROUP + idx
    comb = jnp.sum(jax.nn.one_hot(eid, N_EXPERTS, dtype=jnp.float32)
                   * (pg_top[:, None] * vals)[..., None], axis=1)
    hg = jnp.einsum('nd,edf->nef', t, w_gate)
    hu = jnp.einsum('nd,edf->nef', t, w_up)
    a = jax.nn.silu(hg) * hu * comb.astype(t.dtype)[..., None]
    y = jnp.einsum('nef,efd->nd', a, w_down)
    return y.reshape(B, T, Dm)


def _modulation(cvec, w_mod, b_mod):
    m = jax.nn.silu(cvec) @ w_mod + b_mod
    return jnp.split(m[:, None, :], N_MOD, axis=-1)


def _split_in(p):
    outs = []
    start = 0
    for n in _IN_SIZES:
        outs.append(p[..., start:start + n])
        start += n
    return outs


def _layer(x, cvec, lp, l, angs, prefix):
    (g1, w_mod, b_mod, w_in, w_out, qg, kg, lq1, lk1, lq2, lk2, sub_g,
     g2, w_grp, b_grp, w_exp, b_exp, w_gate, w_up, w_down) = lp
    sh1, sc1, gt1, sh2, sc2, gt2 = _modulation(cvec, w_mod, b_mod)
    B, T, _ = x.shape
    h = _rms(x, g1) * (1 + sc1) + sh1
    p = h @ w_in
    q_a, k_a, v_a, q_d, k_d, v_d, u_f = _split_in(p)
    q_a = _rms(q_a.reshape(B, T, ATTN_Q_HEADS, HEAD_DIM), qg)
    k_a = _rms(k_a.reshape(B, T, ATTN_KV_HEADS, HEAD_DIM), kg)
    v_a = v_a.reshape(B, T, ATTN_KV_HEADS, HEAD_DIM)
    q_d = q_d.reshape(B, T, DIFF_HEADS, 2, DIFF_QK_DIM)
    k_d = k_d.reshape(B, T, DIFF_HEADS, 2, DIFF_QK_DIM)
    v_d = v_d.reshape(B, T, DIFF_HEADS, DIFF_V_DIM)
    u_f = u_f.reshape(B, T, FOURIER_GROUPS, FOURIER_DIM)
    if angs is not None:
        q_a = _rope_2d(q_a, angs[0])
        k_a = _rope_2d(k_a, angs[0])
        q_d = _rope_2d(q_d, angs[1])
        k_d = _rope_2d(k_d, angs[1])
    own = (k_a, v_a, k_d, v_d)
    if prefix is not None:
        K_a = jnp.concatenate([prefix[0], k_a], axis=1)
        V_a = jnp.concatenate([prefix[1], v_a], axis=1)
        K_d = jnp.concatenate([prefix[2], k_d], axis=1)
        V_d = jnp.concatenate([prefix[3], v_d], axis=1)
    else:
        K_a, V_a, K_d, V_d = own
    o_a = _gqa(q_a, K_a, V_a)
    lam_init = 0.8 - 0.6 * math.exp(-0.3 * l)
    lam = (jnp.exp(jnp.sum(lq1.astype(jnp.float32) * lk1.astype(jnp.float32)))
           - jnp.exp(jnp.sum(lq2.astype(jnp.float32) * lk2.astype(jnp.float32))) + lam_init)
    o_d = _rms(_diff_attn(q_d, K_d, V_d, lam), sub_g) * (1.0 - lam_init)
    o_f = _fourier(u_f)
    mix = jnp.concatenate([o_a.reshape(B, T, -1), o_d.reshape(B, T, -1), o_f], axis=-1) @ w_out
    x = x + gt1 * mix
    h2 = _rms(x, g2) * (1 + sc2) + sh2
    x = x + gt2 * _moe(h2, w_grp, b_grp, w_exp, b_exp, w_gate, w_up, w_down)
    return x, own


def setup_inputs(seed: int = 0) -> dict:
    key = jax.random.key(seed)
    ks = iter(jax.random.split(key, 40))

    def nrm(shape, scale=1.0):
        return jax.random.normal(next(ks), shape, jnp.float32) * scale

    def gain(shape):
        return 1.0 + nrm(shape, 0.02)

    D = D_MODEL
    return {
        "x_prompt": nrm((BATCH, SEQ, D)),
        "x_sample": nrm((DEC_BATCH, DEC_SEQ, D)),
        "cache_attn_k": nrm((DEC_BATCH, DEPTH, PAST_LEN, ATTN_KV_HEADS, HEAD_DIM)),
        "cache_attn_v": nrm((DEC_BATCH, DEPTH, PAST_LEN, ATTN_KV_HEADS, HEAD_DIM)),
        "cache_diff_k": nrm((DEC_BATCH, DEPTH, PAST_LEN, DIFF_HEADS, 2, DIFF_QK_DIM)),
        "cache_diff_v": nrm((DEC_BATCH, DEPTH, PAST_LEN, DIFF_HEADS, DIFF_V_DIM)),
        "c": nrm((DEC_BATCH, D)),
        "c_ctx": nrm((D,)),
        "norm1_g": gain((DEPTH, D)),
        "w_mod": nrm((DEPTH, D, N_MOD * D), 0.5 * D ** -0.5),
        "b_mod": nrm((DEPTH, N_MOD * D), 0.02),
        "w_in": nrm((DEPTH, D, IN_WIDTH), D ** -0.5),
        "w_out": nrm((DEPTH, MIX_WIDTH, D), MIX_WIDTH ** -0.5),
        "q_norm_g": gain((DEPTH, HEAD_DIM)),
        "k_norm_g": gain((DEPTH, HEAD_DIM)),
        "lambda_q1": nrm((DEPTH, DIFF_QK_DIM), 0.1),
        "lambda_k1": nrm((DEPTH, DIFF_QK_DIM), 0.1),
        "lambda_q2": nrm((DEPTH, DIFF_QK_DIM), 0.1),
        "lambda_k2": nrm((DEPTH, DIFF_QK_DIM), 0.1),
        "subln_g": gain((DEPTH, DIFF_V_DIM)),
        "norm2_g": gain((DEPTH, D)),
        "w_grp": nrm((DEPTH, D, N_GROUPS), D ** -0.5),
        "b_grp": nrm((DEPTH, N_GROUPS), 0.01),
        "w_exp": nrm((DEPTH, D, N_EXPERTS), D ** -0.5),
        "b_exp": nrm((DEPTH, N_EXPERTS), 0.01),
        "w_gate": nrm((DEPTH, N_EXPERTS, D, D_EXPERT), D ** -0.5),
        "w_up": nrm((DEPTH, N_EXPERTS, D, D_EXPERT), D ** -0.5),
        "w_down": nrm((DEPTH, N_EXPERTS, D_EXPERT, D), D_EXPERT ** -0.5),
        "final_g": gain((D,)),
    }


def reference(x_prompt, x_sample, cache_attn_k, cache_attn_v, cache_diff_k, cache_diff_v, c, c_ctx,
              norm1_g, w_mod, b_mod, w_in, w_out, q_norm_g, k_norm_g,
              lambda_q1, lambda_k1, lambda_q2, lambda_k2, subln_g, norm2_g,
              w_grp, b_grp, w_exp, b_exp, w_gate, w_up, w_down, final_g):
    layer_params = (norm1_g, w_mod, b_mod, w_in, w_out, q_norm_g, k_norm_g,
                    lambda_q1, lambda_k1, lambda_q2, lambda_k2, subln_g, norm2_g,
                    w_grp, b_grp, w_exp, b_exp, w_gate, w_up, w_down)

    xp = x_prompt
    ka_list, va_list, kd_list, vd_list = [], [], [], []
    for l in range(DEPTH):
        lp = tuple(w[l] for w in layer_params)
        xp, (ka, va, kd, vd) = _layer(xp, c_ctx[None, :], lp, l, None, None)
        ka_list.append(ka)
        va_list.append(va)
        kd_list.append(kd)
        vd_list.append(vd)
    y_prompt = _rms(xp, final_g)
    new_attn_k = jnp.stack(ka_list, axis=1)
    new_attn_v = jnp.stack(va_list, axis=1)
    new_diff_k = jnp.stack(kd_list, axis=1)
    new_diff_v = jnp.stack(vd_list, axis=1)

    T = x_sample.shape[1]
    angs = (_axial_angles(T, HEAD_DIM), _axial_angles(T, DIFF_QK_DIM))
    xs = x_sample
    for l in range(DEPTH):
        lp = tuple(w[l] for w in layer_params)
        prefix = (cache_attn_k[:, l], cache_attn_v[:, l], cache_diff_k[:, l], cache_diff_v[:, l])
        xs, _ = _layer(xs, c, lp, l, angs, prefix)
    y_sample = _rms(xs, final_g)

    return (y_prompt, y_sample, new_attn_k, new_attn_v, new_diff_k, new_diff_v)
```

```python
import functools
import math

import jax
import jax.numpy as jnp
import numpy as np
from jax import lax
from jax.experimental import pallas as pl
from jax.experimental.pallas import tpu as pltpu

F32 = jnp.float32
BF16 = jnp.bfloat16

D_MODEL = 1024
BATCH = 32
SEQ = 256
DEPTH = 2
DEC_BATCH = 4
DEC_SEQ = 2048
PAST_LEN = 256
GRID_W = 64
HEAD_DIM = 64
ATTN_Q_HEADS = 6
ATTN_KV_HEADS = 2
DIFF_HEADS = 4
DIFF_QK_DIM = 48
DIFF_V_DIM = 96
FOURIER_GROUPS = 4
FOURIER_DIM = 64
N_GROUPS = 4
EXPERTS_PER_GROUP = 8
N_EXPERTS = N_GROUPS * EXPERTS_PER_GROUP
D_EXPERT = 512
ROPE_THETA = 10000.0
EPS = 1e-6
N_MOD = 6

LANES = 128
SUBLANES = 8

N_PROMPT = BATCH * SEQ
N_SAMPLE = DEC_BATCH * DEC_SEQ
N_TOK = N_PROMPT + N_SAMPLE
QA_W = ATTN_Q_HEADS * HEAD_DIM
KA_W = ATTN_KV_HEADS * HEAD_DIM
DQK_SLOT = 64
QD_W = DIFF_HEADS * 2 * DQK_SLOT
VD_W = DIFF_HEADS * LANES
UF_W = FOURIER_GROUPS * FOURIER_DIM
P_W = QA_W + 2 * KA_W + 2 * QD_W + VD_W + UF_W
ATT_W = QA_W + VD_W
MOD_ROWS = 8
N_ROUTE = N_GROUPS + N_EXPERTS

TM = 512
TQ = 256
TQF = 512
TMM = 256
N_ASSIGN = 2 * N_TOK
P_ROWS = N_ASSIGN + N_EXPERTS * TMM
N_MTILES = P_ROWS // TMM
VMEM_LIMIT = 48 * 1024 * 1024
NEG = -1e30


def _cparams(sem):
    return pltpu.CompilerParams(dimension_semantics=sem, vmem_limit_bytes=VMEM_LIMIT)


def _dot(a, b):
    return jnp.dot(a, b, preferred_element_type=F32)


def _dot_nt(a, b):
    return lax.dot_general(a, b, (((1,), (1,)), ((), ())), preferred_element_type=F32)


def _split(x):
    hi = x.astype(BF16)
    lo = (x - hi.astype(F32)).astype(BF16)
    return hi, lo


def _rms(x, g):
    ms = jnp.mean(x * x, axis=-1, keepdims=True)
    return x * lax.rsqrt(ms + EPS) * g


def _mod_kernel(cv_ref, w_ref, b_ref, o_ref):
    cv = cv_ref[...]
    s = cv / (1.0 + jnp.exp(-cv))
    s_hi, s_lo = _split(s)
    w_hi, w_lo = _split(w_ref[...])
    o_ref[...] = _dot(s_hi, w_hi) + _dot(s_hi, w_lo) + _dot(s_lo, w_hi) + b_ref[...]


def _modulation(cv, w_mod, b_mod):
    tn = 1536
    return pl.pallas_call(
        _mod_kernel,
        out_shape=jax.ShapeDtypeStruct((DEPTH, MOD_ROWS, N_MOD * D_MODEL), F32),
        grid=(DEPTH, N_MOD * D_MODEL // tn),
        in_specs=[
            pl.BlockSpec((MOD_ROWS, D_MODEL), lambda l, n: (0, 0)),
            pl.BlockSpec((None, D_MODEL, tn), lambda l, n: (l, 0, n)),
            pl.BlockSpec((None, 1, tn), lambda l, n: (l, 0, n)),
        ],
        out_specs=pl.BlockSpec((None, MOD_ROWS, tn), lambda l, n: (l, 0, n)),
        compiler_params=_cparams(("arbitrary", "arbitrary")),
        name="modulation",
    )(cv, w_mod, b_mod.reshape(DEPTH, 1, N_MOD * D_MODEL))


def _mod_spec(layer, chunk, row_fn):
    return pl.BlockSpec((None, None, None, 1, D_MODEL),
                        lambda i: (layer, row_fn(i), chunk, 0, 0))


def _merged_row(i):
    n_p = N_PROMPT // TM
    return jnp.where(i < n_p, 0, 1 + ((i - n_p) * TM) // DEC_SEQ)


def _segmean(x2, bd):
    hi, lo = _split(x2)
    return _dot(hi, bd) + _dot(lo, bd)


def _rope(x, c, sa, sb, n):
    w = x.shape[1]
    return x * c + pltpu.roll(x, w - n, 1) * sa + pltpu.roll(x, n, 1) * sb


def _pre_kernel(*refs, rope, caches):
    x_ref, sh_ref, sc_ref, g1_ref, w_ref, gq_ref, gk_ref, bd_ref = refs[:8]
    pos = 8
    if rope:
        ca_ref, saa_ref, sba_ref, cd_ref, sad_ref, sbd_ref = refs[pos:pos + 6]
        pos += 6
    (qa_ref, ka_ref, kas_ref, va_ref, vas_ref, qd_ref, kd_ref, vd_ref, uf_ref) = refs[pos:pos + 9]
    pos += 9
    if caches:
        cka_ref, cva_ref, ckd_ref, cvd_ref = refs[pos:pos + 4]

    x = x_ref[...]
    h = _rms(x, g1_ref[...]) * (1.0 + sc_ref[...]) + sh_ref[...]
    p = _dot(h.astype(BF16), w_ref[...])
    o = 0
    qa = p[:, o:o + QA_W]; o += QA_W
    ka = p[:, o:o + KA_W]; o += KA_W
    va = p[:, o:o + KA_W]; o += KA_W
    qd = p[:, o:o + QD_W]; o += QD_W
    kd = p[:, o:o + QD_W]; o += QD_W
    vd = p[:, o:o + VD_W]; o += VD_W
    uf = p[:, o:o + UF_W]

    bd = bd_ref[...]
    qa = qa * lax.rsqrt(_segmean(qa * qa, bd) + EPS) * gq_ref[...]
    ka = ka * lax.rsqrt(_segmean(ka * ka, bd[:KA_W, :KA_W]) + EPS) * gk_ref[...]
    if caches:
        cka_ref[...] = ka
        cva_ref[...] = va
        ckd_ref[...] = kd
        cvd_ref[...] = vd
    if rope:
        qa = _rope(qa, ca_ref[...], saa_ref[...], sba_ref[...], HEAD_DIM // 4)
        ka = _rope(ka, ca_ref[:, :KA_W], saa_ref[:, :KA_W], sba_ref[:, :KA_W], HEAD_DIM // 4)
        qd = _rope(qd, cd_ref[...], sad_ref[...], sbd_ref[...], DIFF_QK_DIM // 4)
        kd = _rope(kd, cd_ref[...], sad_ref[...], sbd_ref[...], DIFF_QK_DIM // 4)
    qa_ref[...] = (qa * (HEAD_DIM ** -0.5)).astype(BF16)
    ka_ref[...] = ka.astype(BF16)
    kas_ref[...] = pltpu.roll(ka, HEAD_DIM, 1).astype(BF16)
    va_ref[...] = va.astype(BF16)
    vas_ref[...] = pltpu.roll(va, HEAD_DIM, 1).astype(BF16)
    qd_ref[...] = (qd * (DIFF_QK_DIM ** -0.5)).astype(BF16)
    kd_ref[...] = kd.astype(BF16)
    vd_ref[...] = vd.astype(BF16)
    uf_ref[...] = uf.astype(BF16)


def _pre_attention(layer, x_half, mod5, g1, w_in_r, gq, gk, bd64, rope_tabs, *, sample):
    n_rows = N_SAMPLE if sample else N_PROMPT
    n_tiles = n_rows // TM
    if sample:
        row_fn = lambda i: 1 + (i * TM) // DEC_SEQ
    else:
        row_fn = lambda i: 0
    tile = lambda w: pl.BlockSpec((TM, w), lambda i: (i, 0))
    const2 = lambda a: pl.BlockSpec(a.shape, lambda i: (0, 0))
    in_specs = [
        pl.BlockSpec((TM, D_MODEL), lambda i: (i, 0)),
        _mod_spec(layer, 0, row_fn),
        _mod_spec(layer, 1, row_fn),
        pl.BlockSpec((None, 1, D_MODEL), lambda i: (layer, 0, 0)),
        pl.BlockSpec((None, D_MODEL, P_W), lambda i: (layer, 0, 0)),
        pl.BlockSpec((None, 1, QA_W), lambda i: (layer, 0, 0)),
        pl.BlockSpec((None, 1, KA_W), lambda i: (layer, 0, 0)),
        const2(bd64),
    ]
    args = [x_half, mod5, mod5, g1, w_in_r, gq, gk, bd64]
    if sample:
        tpb = DEC_SEQ // TM
        for t in rope_tabs:
            in_specs.append(pl.BlockSpec((TM, t.shape[1]), lambda i: (i % tpb, 0)))
            args.append(t)
    widths = [QA_W, KA_W, KA_W, KA_W, KA_W, QD_W, QD_W, VD_W, UF_W]
    out_shape = [jax.ShapeDtypeStruct((n_rows, w), BF16) for w in widths]
    out_specs = [tile(w) for w in widths]
    if not sample:
        for w in (KA_W, KA_W, QD_W, VD_W):
            out_shape.append(jax.ShapeDtypeStruct((n_rows, w), F32))
            out_specs.append(tile(w))
    return pl.pallas_call(
        functools.partial(_pre_kernel, rope=sample, caches=not sample),
        out_shape=out_shape,
        grid=(n_tiles,),
        in_specs=in_specs,
        out_specs=out_specs,
        compiler_params=_cparams(("arbitrary",)),
        name="pre_attention_sample" if sample else "pre_attention_prompt",
    )(*args)


def _softmax_pv(qcol, k_pieces, v_pieces):
    s = [_dot_nt(qcol, k) for k in k_pieces]
    m = s[0].max(axis=-1, keepdims=True)
    for si in s[1:]:
        m = jnp.maximum(m, si.max(axis=-1, keepdims=True))
    acc = None
    den = None
    for si, v in zip(s, v_pieces):
        e = jnp.exp(si - m)
        d = e.sum(axis=-1, keepdims=True)
        o = _dot(e.astype(BF16), v)
        acc = o if acc is None else acc + o
        den = d if den is None else den + d
    return acc * (1.0 / den)


def _attn_kernel(*refs, n_pieces, lam_init):
    qa_ref, qd_ref = refs[:2]
    pos = 2
    pieces = []
    for _ in range(n_pieces):
        pieces.append(refs[pos:pos + 6])
        pos += 6
    lam_ref, subg_ref, o_ref = refs[pos:pos + 3]

    lamv = lam_ref[...]
    s1 = jnp.sum(lamv[0:1] * lamv[1:2], axis=-1, keepdims=True)
    s2 = jnp.sum(lamv[2:3] * lamv[3:4], axis=-1, keepdims=True)
    lam = jnp.exp(s1) - jnp.exp(s2) + lam_init

    def halves(x):
        lo = lax.broadcasted_iota(jnp.int32, x.shape, 1) < HEAD_DIM
        z = jnp.zeros_like(x)
        return jnp.where(lo, x, z), jnp.where(lo, z, x)

    k_var, v_var = [], []
    for (ka_ref, kas_ref, va_ref, vas_ref, _, _) in pieces:
        a_k, b_k = halves(ka_ref[...])
        c_k, d_k = halves(kas_ref[...])
        a_v, b_v = halves(va_ref[...])
        c_v, d_v = halves(vas_ref[...])
        k_var.append((a_k, d_k, a_k, b_k, c_k, b_k))
        v_var.append((a_v, d_v, a_v, b_v, c_v, b_v))
    for j in range(ATTN_Q_HEADS // 2):
        qcol = qa_ref[:, j * LANES:(j + 1) * LANES]
        col = None
        for h in (2 * j, 2 * j + 1):
            o = _softmax_pv(qcol, [kv[h] for kv in k_var], [vv[h] for vv in v_var])
            col = o if col is None else col + o
        o_ref[:, j * LANES:(j + 1) * LANES] = col.astype(BF16)

    for h in range(DIFF_HEADS):
        sl = slice(h * LANES, (h + 1) * LANES)
        qcol = qd_ref[:, sl]
        k1, k2, vs = [], [], []
        for (_, _, _, _, kd_ref, vd_ref) in pieces:
            a, b = halves(kd_ref[:, sl])
            k1.append(a)
            k2.append(b)
            vs.append(vd_ref[:, sl])
        od = _softmax_pv(qcol, k1, vs) - lam * _softmax_pv(qcol, k2, vs)
        ms = jnp.sum(od * od, axis=-1, keepdims=True) * (1.0 / DIFF_V_DIM)
        od = od * lax.rsqrt(ms + EPS) * subg_ref[:, sl] * (1.0 - lam_init)
        o_ref[:, QA_W + h * LANES:QA_W + (h + 1) * LANES] = od.astype(BF16)


def _attention(layer, pre, prefix, lam_in, subg, *, sample):
    qa, ka, kas, va, vas, qd, kd, vd = pre[:8]
    lam_init = 0.8 - 0.6 * math.exp(-0.3 * layer)
    if sample:
        nb, seq, nq = DEC_BATCH, DEC_SEQ, DEC_SEQ // TQ
    else:
        nb, seq, nq = BATCH, SEQ, SEQ // TQ
    qspec = lambda w: pl.BlockSpec((TQ, w), lambda b, q: (b * nq + q, 0))
    kvspec = lambda w: pl.BlockSpec((seq, w), lambda b, q: (b, 0))
    in_specs = [qspec(QA_W), qspec(QD_W)]
    args = [qa, qd]
    n_pieces = 1
    if sample:
        n_pieces = 2
        for a in prefix:
            in_specs.append(pl.BlockSpec((None, None, PAST_LEN, a.shape[-1]),
                                         lambda b, q: (b, layer, 0, 0)))
            args.append(a)
    for a, w in ((ka, KA_W), (kas, KA_W), (va, KA_W), (vas, KA_W), (kd, QD_W), (vd, VD_W)):
        in_specs.append(kvspec(w))
        args.append(a)
    in_specs += [pl.BlockSpec((None, 4, DIFF_QK_DIM), lambda b, q: (layer, 0, 0)),
                 pl.BlockSpec((None, 1, VD_W), lambda b, q: (layer, 0, 0))]
    args += [lam_in, subg]
    return pl.pallas_call(
        functools.partial(_attn_kernel, n_pieces=n_pieces, lam_init=lam_init),
        out_shape=jax.ShapeDtypeStruct((nb * seq, ATT_W), BF16),
        grid=(nb, nq),
        in_specs=in_specs,
        out_specs=pl.BlockSpec((TQ, ATT_W), lambda b, q: (b * nq + q, 0)),
        compiler_params=_cparams(("arbitrary", "arbitrary")),
        name="attention_sample" if sample else "attention_prompt",
    )(*args)


def _fourier_kernel(u_ref, cs_ref, tw_ref, o_ref, ab_ref, *, seq, norm):
    @pl.when(pl.program_id(1) == 0)
    def _():
        ab = _dot(u_ref[...], cs_ref[...])
        ab_ref[0:seq, :] = ab[:, :UF_W].astype(BF16)
        ab_ref[seq:2 * seq, :] = ab[:, UF_W:].astype(BF16)

    o_ref[...] = (_dot(tw_ref[...], ab_ref[...]) * norm).astype(BF16)


def _fourier(uf, cs64, tw, *, sample):
    if sample:
        nb, seq, tq = DEC_BATCH, DEC_SEQ, TQF
    else:
        nb, seq, tq = BATCH, SEQ, SEQ
    nq = seq // tq
    in_specs = [pl.BlockSpec((seq, UF_W), lambda b, q: (b, 0)),
                pl.BlockSpec(cs64.shape, lambda b, q: (0, 0)),
                pl.BlockSpec((tq, 2 * seq), lambda b, q: (q, 0))]
    return pl.pallas_call(
        functools.partial(_fourier_kernel, seq=seq, norm=(seq * FOURIER_DIM) ** -0.5),
        out_shape=jax.ShapeDtypeStruct((nb * seq, UF_W), BF16),
        grid=(nb, nq),
        in_specs=in_specs,
        out_specs=pl.BlockSpec((tq, UF_W), lambda b, q: (b * nq + q, 0)),
        scratch_shapes=[pltpu.VMEM((2 * seq, UF_W), BF16)],
        compiler_params=_cparams(("arbitrary", "arbitrary")),
        name="fourier_sample" if sample else "fourier_prompt",
    )(uf, cs64, tw)


def _post_kernel(xp_ref, xs_ref, attp_ref, atts_ref, fop_ref, fos_ref, woa_ref, wof_ref,
                 gt1_ref, sh2_ref, sc2_ref, g2_ref, wrh_ref, wrl_ref, br_ref, ltri_ref,
                 xn_ref, h2_ref, route_ref, cnt_ref, carry_ref):
    @pl.when(pl.program_id(0) == 0)
    def _():
        carry_ref[...] = jnp.zeros_like(carry_ref)

    is_prompt = pl.program_id(0) < N_PROMPT // TM
    pick = lambda p_ref, s_ref: jnp.where(is_prompt, p_ref[...], s_ref[...])
    mix = (_dot(pick(attp_ref, atts_ref), woa_ref[...])
           + _dot(pick(fop_ref, fos_ref), wof_ref[...]))
    xn = pick(xp_ref, xs_ref) + gt1_ref[...] * mix
    xn_ref[...] = xn
    h2 = _rms(xn, g2_ref[...]) * (1.0 + sc2_ref[...]) + sh2_ref[...]
    h2_ref[...] = h2

    h_hi, h_lo = _split(h2)
    wrh = wrh_ref[...]
    logits = _dot(h_hi, wrh) + _dot(h_hi, wrl_ref[...]) + _dot(h_lo, wrh) + br_ref[...]
    lane = lax.broadcasted_iota(jnp.int32, logits.shape, 1).astype(F32)
    big = float(LANES)

    def first_max(v):
        m = v.max(axis=-1, keepdims=True)
        idx = jnp.where(v == m, lane, big).min(axis=-1, keepdims=True)
        return m, idx

    lg = jnp.where(lane < N_GROUPS, logits, NEG)
    mg, gi = first_max(lg)
    pg_top = 1.0 / jnp.exp(lg - mg).sum(axis=-1, keepdims=True)
    e_lo = N_GROUPS + gi * EXPERTS_PER_GROUP
    le = jnp.where((lane >= e_lo) & (lane < e_lo + EXPERTS_PER_GROUP), logits, NEG)
    m0, i0 = first_max(le)
    m1, i1 = first_max(jnp.where(lane == i0, NEG, le))
    e1 = jnp.exp(m1 - m0)
    inv = 1.0 / (1.0 + e1)
    w0 = pg_top * inv
    w1 = pg_top * (e1 * inv)

    hot0 = lane == i0
    hot1 = lane == i1
    onehot = jnp.where(hot0 | hot1, 1.0, 0.0)
    before = _dot(ltri_ref[...], onehot.astype(BF16)) + carry_ref[0:1, :]
    r0 = jnp.where(hot0, before, 0.0).sum(axis=-1, keepdims=True)
    r1 = jnp.where(hot1, before, 0.0).sum(axis=-1, keepdims=True)
    carry = carry_ref[...] + onehot.sum(axis=0, keepdims=True)
    carry_ref[...] = carry
    cnt_ref[...] = carry

    route = jnp.where(lane == 0, w0, 0.0)
    route = jnp.where(lane == 1, w1, route)
    route = jnp.where(lane == 2, i0 - N_GROUPS, route)
    route = jnp.where(lane == 3, i1 - N_GROUPS, route)
    route = jnp.where(lane == 4, r0, route)
    route = jnp.where(lane == 5, r1, route)
    route_ref[...] = route


def _post_attention(layer, x_pair, att_pair, fo_pair, woa, wof, mod5, g2, wr_hi, wr_lo, br, ltri):
    n_p = N_PROMPT // TM
    tile = lambda w: pl.BlockSpec((TM, w), lambda i: (i, 0))
    ptile = lambda w: pl.BlockSpec((TM, w), lambda i: (jnp.minimum(i, n_p - 1), 0))
    stile = lambda w: pl.BlockSpec((TM, w), lambda i: (jnp.maximum(i - n_p, 0), 0))
    lay3 = lambda a: pl.BlockSpec((None,) + a.shape[1:], lambda i: (layer, 0, 0))
    return pl.pallas_call(
        _post_kernel,
        out_shape=[jax.ShapeDtypeStruct((N_TOK, D_MODEL), F32),
                   jax.ShapeDtypeStruct((N_TOK, D_MODEL), F32),
                   jax.ShapeDtypeStruct((N_TOK, LANES), F32),
                   jax.ShapeDtypeStruct((SUBLANES, LANES), F32)],
        grid=(N_TOK // TM,),
        in_specs=[ptile(D_MODEL), stile(D_MODEL), ptile(ATT_W), stile(ATT_W), ptile(UF_W), stile(UF_W),
                  lay3(woa), lay3(wof),
                  _mod_spec(layer, 2, _merged_row), _mod_spec(layer, 3, _merged_row),
                  _mod_spec(layer, 4, _merged_row), lay3(g2), lay3(wr_hi), lay3(wr_lo), lay3(br),
                  pl.BlockSpec(ltri.shape, lambda i: (0, 0))],
        out_specs=[tile(D_MODEL), tile(D_MODEL), tile(LANES),
                   pl.BlockSpec((SUBLANES, LANES), lambda i: (0, 0))],
        scratch_shapes=[pltpu.VMEM((SUBLANES, LANES), F32)],
        compiler_params=_cparams(("arbitrary",)),
        name="post_attention_router",
    )(*x_pair, *att_pair, *fo_pair, woa, wof, mod5, mod5, mod5, g2, wr_hi, wr_lo, br, ltri)


def _row_copy(src_hbm, dst_hbm, src_row, dst_row, sem):
    return pltpu.make_async_copy(src_hbm.at[pl.ds(src_row, 1)], dst_hbm.at[pl.ds(dst_row, 1)], sem)


def _moe_kernel(te_ref, act_ref, src_ref, dst_ref, h2_hbm, wg_ref, wu_ref, wd_ref, y2_hbm,
                xg_ref, yo_ref, sem):
    j = pl.program_id(0)

    @pl.when(j == 0)
    def _():
        yo_ref[...] = jnp.zeros_like(yo_ref)
        spare = pltpu.make_async_copy(yo_ref, y2_hbm.at[pl.ds(N_ASSIGN, TMM)], sem.at[1])
        spare.start()
        spare.wait()

    @pl.when(act_ref[j] == 1)
    def _():
        def gather_start(r, c):
            _row_copy(h2_hbm, xg_ref, src_ref[0, 0, r], r, sem.at[0]).start()
            return c

        def gather_wait(r, c):
            _row_copy(h2_hbm, xg_ref, 0, r, sem.at[0]).wait()
            return c

        lax.fori_loop(0, TMM, gather_start, 0, unroll=8)
        lax.fori_loop(0, TMM, gather_wait, 0, unroll=8)

        x = xg_ref[...].astype(BF16)
        hg = _dot(x, wg_ref[...].astype(BF16))
        hu = _dot(x, wu_ref[...].astype(BF16))
        a = (hg / (1.0 + jnp.exp(-hg))) * hu
        yo_ref[...] = _dot(a.astype(BF16), wd_ref[...].astype(BF16))

        def scatter_start(r, c):
            _row_copy(yo_ref, y2_hbm, r, dst_ref[0, 0, r], sem.at[1]).start()
            return c

        def scatter_wait(r, c):
            _row_copy(yo_ref, y2_hbm, r, 0, sem.at[1]).wait()
            return c

        lax.fori_loop(0, TMM, scatter_start, 0, unroll=8)
        lax.fori_loop(0, TMM, scatter_wait, 0, unroll=8)


def _moe(layer, tile_expert, tile_active, src_tok, dst_row, h2, w_gate, w_up, w_down):
    idx_spec = pl.BlockSpec((1, 1, TMM), lambda j, te, act: (j, 0, 0), memory_space=pltpu.SMEM)
    grid_spec = pltpu.PrefetchScalarGridSpec(
        num_scalar_prefetch=2,
        grid=(N_MTILES,),
        in_specs=[
            idx_spec, idx_spec,
            pl.BlockSpec(memory_space=pl.ANY),
            pl.BlockSpec((None, None, D_MODEL, D_EXPERT), lambda j, te, act: (layer, te[j], 0, 0)),
            pl.BlockSpec((None, None, D_MODEL, D_EXPERT), lambda j, te, act: (layer, te[j], 0, 0)),
            pl.BlockSpec((None, None, D_EXPERT, D_MODEL), lambda j, te, act: (layer, te[j], 0, 0)),
        ],
        out_specs=pl.BlockSpec(memory_space=pl.ANY),
        scratch_shapes=[pltpu.VMEM((TMM, D_MODEL), F32), pltpu.VMEM((TMM, D_MODEL), F32),
                        pltpu.SemaphoreType.DMA((2,))],
    )
    return pl.pallas_call(
        _moe_kernel,
        out_shape=jax.ShapeDtypeStruct((N_ASSIGN + TMM, D_MODEL), F32),
        grid_spec=grid_spec,
        compiler_params=_cparams(("arbitrary",)),
        name="moe_grouped_matmul",
    )(tile_expert, tile_active, src_tok.reshape(N_MTILES, 1, TMM), dst_row.reshape(N_MTILES, 1, TMM),
      h2, w_gate, w_up, w_down)


def _sort_plan(route, counts):
    cnt = counts[0, N_GROUPS:N_GROUPS + N_EXPERTS].astype(jnp.int32)
    padded = ((cnt + TMM - 1) // TMM) * TMM
    ends = jnp.cumsum(padded)
    starts = ends - padded
    expert = route[:, 2:4].astype(jnp.int32)
    rank = route[:, 4:6].astype(jnp.int32)
    pos = (starts[expert] + rank).reshape(-1)
    assign = jnp.arange(N_ASSIGN, dtype=jnp.int32)
    rows = jnp.arange(P_ROWS, dtype=jnp.int32)
    src_tok = jnp.zeros((P_ROWS,), jnp.int32).at[pos].set(assign // 2)
    dst_row = (N_ASSIGN + rows % TMM).at[pos].set(assign)
    tile_start = jnp.arange(N_MTILES, dtype=jnp.int32) * TMM
    active = (tile_start < ends[-1]).astype(jnp.int32)
    te = jnp.searchsorted(ends, tile_start, side="right").astype(jnp.int32)
    last = jnp.searchsorted(ends, ends[-1] - 1, side="right").astype(jnp.int32)
    te = jnp.where(active == 1, te, last)
    return te, active, src_tok, dst_row


def _combine_kernel(xn_ref, y_ref, route_ref, gt2_ref, *rest, final):
    o_ref = rest[-1]
    r = route_ref[...]
    y = r[:, 0:1] * y_ref[:, :D_MODEL] + r[:, 1:2] * y_ref[:, D_MODEL:]
    x = xn_ref[...] + gt2_ref[...] * y
    if final:
        x = _rms(x, rest[0][...])
    o_ref[...] = x


def _combine(layer, xn, y2, route, mod5, final_g, *, sample):
    final = final_g is not None
    y2v = y2.reshape((N_ASSIGN + TMM) // 2, 2 * D_MODEL)
    if sample:
        n_rows, off = N_SAMPLE, N_PROMPT // TM
        row_fn = lambda i: 1 + (i * TM) // DEC_SEQ
    else:
        n_rows, off, row_fn = N_PROMPT, 0, (lambda i: 0)
    tile = lambda w: pl.BlockSpec((TM, w), lambda i: (i + off, 0))
    in_specs = [tile(D_MODEL), tile(2 * D_MODEL), tile(LANES), _mod_spec(layer, 5, row_fn)]
    args = [xn, y2v, route, mod5]
    if final:
        in_specs.append(pl.BlockSpec((1, D_MODEL), lambda i: (0, 0)))
        args.append(final_g)
    return pl.pallas_call(
        functools.partial(_combine_kernel, final=final),
        out_shape=jax.ShapeDtypeStruct((n_rows, D_MODEL), F32),
        grid=(n_rows // TM,),
        in_specs=in_specs,
        out_specs=pl.BlockSpec((TM, D_MODEL), lambda i: (i, 0)),
        compiler_params=_cparams(("arbitrary",)),
        name="moe_combine_" + ("sample" if sample else "prompt") + ("_final" if final else ""),
    )(*args)


def _pad_segments(w, axis, nseg, seg, segp):
    shape = w.shape
    w = w.reshape(shape[:axis] + (nseg, seg) + shape[axis + 1:])
    pad = [(0, 0)] * w.ndim
    pad[axis + 1] = (0, segp - seg)
    w = jnp.pad(w, pad)
    return w.reshape(shape[:axis] + (nseg * segp,) + shape[axis + 1:])


def _axial_angles(seq, dim):
    rows = seq // GRID_W
    r = jnp.repeat(jnp.arange(rows), GRID_W).astype(F32)
    col = jnp.tile(jnp.arange(GRID_W), rows).astype(F32)
    n = dim // 4
    freqs = ROPE_THETA ** (-jnp.arange(n, dtype=F32) / n)
    return r[:, None] * freqs, col[:, None] * freqs


def _rope_tables(seq, dim, slot, nslots):
    ang_r, ang_c = _axial_angles(seq, dim)
    n = dim // 4
    z = jnp.zeros((seq, n), F32)
    pad = jnp.zeros((seq, slot - dim), F32)
    cr, sr, cc, sc = jnp.cos(ang_r), jnp.sin(ang_r), jnp.cos(ang_c), jnp.sin(ang_c)
    c = jnp.concatenate([cr, cr, cc, cc, pad], axis=1)
    sa = jnp.concatenate([-sr, z, -sc, z, pad], axis=1)
    sb = jnp.concatenate([z, sr, z, sc, pad], axis=1)
    return [jnp.tile(t, (1, nslots)) for t in (c, sa, sb)]


def _dft_tables(seq):
    t = jnp.arange(seq, dtype=jnp.int32)
    ang = ((t[:, None] * t[None, :]) % seq).astype(F32) * (2.0 * math.pi / seq)
    return jnp.concatenate([jnp.cos(ang), -jnp.sin(ang)], axis=1).astype(BF16)


def _channel_dft():
    c = jnp.arange(FOURIER_DIM, dtype=jnp.int32)
    ang = ((c[:, None] * c[None, :]) % FOURIER_DIM).astype(F32) * (2.0 * math.pi / FOURIER_DIM)
    eye = jnp.eye(FOURIER_GROUPS, dtype=F32)
    return jnp.concatenate([jnp.kron(eye, jnp.cos(ang)), jnp.kron(eye, jnp.sin(ang))],
                           axis=1).astype(BF16)


def kernel(x_prompt, x_sample, cache_attn_k, cache_attn_v, cache_diff_k, cache_diff_v, c, c_ctx,
           norm1_g, w_mod, b_mod, w_in, w_out, q_norm_g, k_norm_g,
           lambda_q1, lambda_k1, lambda_q2, lambda_k2, subln_g, norm2_g,
           w_grp, b_grp, w_exp, b_exp, w_gate, w_up, w_down, final_g):
    cv = jnp.concatenate([c_ctx[None, :], c, jnp.zeros((MOD_ROWS - 1 - DEC_BATCH, D_MODEL), F32)], axis=0)
    mod5 = _modulation(cv, w_mod, b_mod).reshape(DEPTH, MOD_ROWS, N_MOD, 1, D_MODEL)

    o0, o1, o2, o3, o4 = QA_W + 2 * KA_W, QA_W + 2 * KA_W + 384, QA_W + 2 * KA_W + 768, 1792, 2048
    w_in_r = jnp.concatenate([
        w_in[..., :o0],
        _pad_segments(w_in[..., o0:o1], 2, 2 * DIFF_HEADS, DIFF_QK_DIM, DQK_SLOT),
        _pad_segments(w_in[..., o1:o2], 2, 2 * DIFF_HEADS, DIFF_QK_DIM, DQK_SLOT),
        _pad_segments(w_in[..., o2:o3], 2, DIFF_HEADS, DIFF_V_DIM, LANES),
        w_in[..., o3:o4]], axis=-1).astype(BF16)
    woa = jnp.concatenate([w_out[:, :QA_W],
                           _pad_segments(w_out[:, QA_W:QA_W + DIFF_HEADS * DIFF_V_DIM], 1,
                                         DIFF_HEADS, DIFF_V_DIM, LANES)], axis=1).astype(BF16)
    wof = w_out[:, QA_W + DIFF_HEADS * DIFF_V_DIM:].astype(BF16)
    g1 = norm1_g.reshape(DEPTH, 1, D_MODEL)
    g2 = norm2_g.reshape(DEPTH, 1, D_MODEL)
    gq = jnp.tile(q_norm_g, (1, ATTN_Q_HEADS)).reshape(DEPTH, 1, QA_W)
    gk = jnp.tile(k_norm_g, (1, ATTN_KV_HEADS)).reshape(DEPTH, 1, KA_W)
    subg = _pad_segments(jnp.tile(subln_g, (1, DIFF_HEADS)), 1, DIFF_HEADS, DIFF_V_DIM, LANES
                         ).reshape(DEPTH, 1, VD_W)
    lam_in = jnp.stack([lambda_q1, lambda_k1, lambda_q2, lambda_k2], axis=1)
    w_r = jnp.concatenate([w_grp, w_exp, jnp.zeros((DEPTH, D_MODEL, LANES - N_ROUTE), F32)], axis=-1)
    wr_hi = w_r.astype(BF16)
    wr_lo = (w_r - wr_hi.astype(F32)).astype(BF16)
    br = jnp.concatenate([b_grp, b_exp, jnp.zeros((DEPTH, LANES - N_ROUTE), F32)], axis=-1
                         ).reshape(DEPTH, 1, LANES)
    seg = np.arange(QA_W) // HEAD_DIM
    bd64 = jnp.asarray((seg[:, None] == seg[None, :]).astype(np.float32) / HEAD_DIM, dtype=BF16)
    ltri = jnp.asarray(np.tril(np.ones((TM, TM), np.float32), -1), dtype=BF16)
    rope_tabs = (_rope_tables(DEC_SEQ, HEAD_DIM, HEAD_DIM, ATTN_Q_HEADS)
                 + _rope_tables(DEC_SEQ, DIFF_QK_DIM, DQK_SLOT, 2 * DIFF_HEADS))
    cs64 = _channel_dft()
    tw_p = _dft_tables(SEQ)
    tw_s = _dft_tables(DEC_SEQ)
    pka = cache_attn_k.reshape(DEC_BATCH, DEPTH, PAST_LEN, KA_W)
    pva = cache_attn_v.reshape(DEC_BATCH, DEPTH, PAST_LEN, KA_W)
    pkd = _pad_segments(cache_diff_k.reshape(DEC_BATCH, DEPTH, PAST_LEN, 2 * DIFF_HEADS * DIFF_QK_DIM),
                        3, 2 * DIFF_HEADS, DIFF_QK_DIM, DQK_SLOT)
    pvd = _pad_segments(cache_diff_v.reshape(DEC_BATCH, DEPTH, PAST_LEN, DIFF_HEADS * DIFF_V_DIM),
                        3, DIFF_HEADS, DIFF_V_DIM, LANES)
    prefix = [pka.astype(BF16), jnp.roll(pka, HEAD_DIM, axis=-1).astype(BF16),
              pva.astype(BF16), jnp.roll(pva, HEAD_DIM, axis=-1).astype(BF16),
              pkd.astype(BF16), pvd.astype(BF16)]

    caches = []
    x_p = x_prompt.reshape(N_PROMPT, D_MODEL)
    x_s = x_sample.reshape(N_SAMPLE, D_MODEL)
    for layer in range(DEPTH):
        pre_p = _pre_attention(layer, x_p, mod5, g1, w_in_r, gq, gk, bd64, None, sample=False)
        pre_s = _pre_attention(layer, x_s, mod5, g1, w_in_r, gq, gk, bd64, rope_tabs, sample=True)
        caches.append(pre_p[9:])
        att_p = _attention(layer, pre_p, None, lam_in, subg, sample=False)
        att_s = _attention(layer, pre_s, prefix, lam_in, subg, sample=True)
        fo_p = _fourier(pre_p[8], cs64, tw_p, sample=False)
        fo_s = _fourier(pre_s[8], cs64, tw_s, sample=True)
        xn, h2, route, counts = _post_attention(layer, (x_p, x_s), (att_p, att_s), (fo_p, fo_s),
                                                woa, wof, mod5, g2, wr_hi, wr_lo, br, ltri)
        te, active, src_tok, dst_row = _sort_plan(route, counts)
        y2 = _moe(layer, te, active, src_tok, dst_row, h2, w_gate, w_up, w_down)
        fg = final_g.reshape(1, D_MODEL) if layer + 1 == DEPTH else None
        x_p = _combine(layer, xn, y2, route, mod5, fg, sample=False)
        x_s = _combine(layer, xn, y2, route, mod5, fg, sample=True)
    y_prompt, y_sample = x_p, x_s

    def stack(i, shape, keep):
        arrs = [cl[i].reshape(shape)[..., :keep] for cl in caches]
        return jnp.stack(arrs, axis=1)

    new_attn_k = stack(0, (BATCH, SEQ, ATTN_KV_HEADS, HEAD_DIM), HEAD_DIM)
    new_attn_v = stack(1, (BATCH, SEQ, ATTN_KV_HEADS, HEAD_DIM), HEAD_DIM)
    new_diff_k = stack(2, (BATCH, SEQ, DIFF_HEADS, 2, DQK_SLOT), DIFF_QK_DIM)
    new_diff_v = stack(3, (BATCH, SEQ, DIFF_HEADS, LANES), DIFF_V_DIM)
    return (y_prompt.reshape(BATCH, SEQ, D_MODEL), y_sample.reshape(DEC_BATCH, DEC_SEQ, D_MODEL),
            new_attn_k, new_attn_v, new_diff_k, new_diff_v)
```

```python
import functools
import math

import jax
import jax.numpy as jnp
import numpy as np
from jax import lax
from jax.experimental import pallas as pl
from jax.experimental.pallas import tpu as pltpu

F32 = jnp.float32
BF16 = jnp.bfloat16

D_MODEL = 1024
BATCH = 32
SEQ = 256
DEPTH = 2
DEC_BATCH = 4
DEC_SEQ = 2048
PAST_LEN = 256
GRID_W = 64
HEAD_DIM = 64
ATTN_Q_HEADS = 6
ATTN_KV_HEADS = 2
DIFF_HEADS = 4
DIFF_QK_DIM = 48
DIFF_V_DIM = 96
FOURIER_GROUPS = 4
FOURIER_DIM = 64
N_GROUPS = 4
EXPERTS_PER_GROUP = 8
N_EXPERTS = N_GROUPS * EXPERTS_PER_GROUP
D_EXPERT = 512
ROPE_THETA = 10000.0
EPS = 1e-6
N_MOD = 6

LANES = 128
SUBLANES = 8

N_PROMPT = BATCH * SEQ
N_SAMPLE = DEC_BATCH * DEC_SEQ
N_TOK = N_PROMPT + N_SAMPLE
QA_W = ATTN_Q_HEADS * HEAD_DIM
KA_W = ATTN_KV_HEADS * HEAD_DIM
DQK_SLOT = 64
QD_W = DIFF_HEADS * 2 * DQK_SLOT
VD_W = DIFF_HEADS * LANES
UF_W = FOURIER_GROUPS * FOURIER_DIM
P_W = QA_W + 2 * KA_W + 2 * QD_W + VD_W + UF_W
ATT_W = QA_W + VD_W
MOD_ROWS = 8
N_ROUTE = N_GROUPS + N_EXPERTS

TM = 512
TQ = 256
TQF = 512
TMM = 256
TD = 2048
N_ASSIGN = 2 * N_TOK
P_ROWS = N_ASSIGN + N_EXPERTS * TMM
N_MTILES = P_ROWS // TMM
VMEM_LIMIT = 48 * 1024 * 1024
NEG = -1e30


def _cparams(sem):
    return pltpu.CompilerParams(dimension_semantics=sem, vmem_limit_bytes=VMEM_LIMIT)


def _dot(a, b):
    return jnp.dot(a, b, preferred_element_type=F32)


def _dot_nt(a, b):
    return lax.dot_general(a, b, (((1,), (1,)), ((), ())), preferred_element_type=F32)


def _split(x):
    hi = x.astype(BF16)
    lo = (x - hi.astype(F32)).astype(BF16)
    return hi, lo


def _rms(x, g):
    ms = jnp.mean(x * x, axis=-1, keepdims=True)
    return x * lax.rsqrt(ms + EPS) * g


def _mod_kernel(cv_ref, w_ref, b_ref, o_ref):
    cv = cv_ref[...]
    s = cv / (1.0 + jnp.exp(-cv))
    s_hi, s_lo = _split(s)
    w_hi, w_lo = _split(w_ref[...])
    o_ref[...] = _dot(s_hi, w_hi) + _dot(s_hi, w_lo) + _dot(s_lo, w_hi) + b_ref[...]


def _modulation(cv, w_mod, b_mod):
    tn = 1536
    return pl.pallas_call(
        _mod_kernel,
        out_shape=jax.ShapeDtypeStruct((DEPTH, MOD_ROWS, N_MOD * D_MODEL), F32),
        grid=(DEPTH, N_MOD * D_MODEL // tn),
        in_specs=[
            pl.BlockSpec((MOD_ROWS, D_MODEL), lambda l, n: (0, 0)),
            pl.BlockSpec((None, D_MODEL, tn), lambda l, n: (l, 0, n)),
            pl.BlockSpec((None, 1, tn), lambda l, n: (l, 0, n)),
        ],
        out_specs=pl.BlockSpec((None, MOD_ROWS, tn), lambda l, n: (l, 0, n)),
        compiler_params=_cparams(("arbitrary", "arbitrary")),
        name="modulation",
    )(cv, w_mod, b_mod.reshape(DEPTH, 1, N_MOD * D_MODEL))


def _mod_spec(layer, chunk, row_fn):
    return pl.BlockSpec((None, None, None, 1, D_MODEL),
                        lambda i: (layer, row_fn(i), chunk, 0, 0))


def _merged_row(i):
    n_p = N_PROMPT // TM
    return jnp.where(i < n_p, 0, 1 + ((i - n_p) * TM) // DEC_SEQ)


def _segmean(x2, bd):
    hi, lo = _split(x2)
    return _dot(hi, bd) + _dot(lo, bd)


def _rope(x, c, sa, sb, n):
    w = x.shape[1]
    return x * c + pltpu.roll(x, w - n, 1) * sa + pltpu.roll(x, n, 1) * sb


def _pre_kernel(*refs, rope, caches):
    x_ref, sh_ref, sc_ref, g1_ref, w_ref, gq_ref, gk_ref, bd_ref = refs[:8]
    pos = 8
    if rope:
        ca_ref, saa_ref, sba_ref, cd_ref, sad_ref, sbd_ref = refs[pos:pos + 6]
        pos += 6
    (qa_ref, ka_ref, kas_ref, va_ref, vas_ref, qd_ref, kd_ref, vd_ref, uf_ref) = refs[pos:pos + 9]
    pos += 9
    if caches:
        cka_ref, cva_ref, ckd_ref, cvd_ref = refs[pos:pos + 4]

    x = x_ref[...]
    h = _rms(x, g1_ref[...]) * (1.0 + sc_ref[...]) + sh_ref[...]
    p = _dot(h.astype(BF16), w_ref[...])
    o = 0
    qa = p[:, o:o + QA_W]; o += QA_W
    ka = p[:, o:o + KA_W]; o += KA_W
    va = p[:, o:o + KA_W]; o += KA_W
    qd = p[:, o:o + QD_W]; o += QD_W
    kd = p[:, o:o + QD_W]; o += QD_W
    vd = p[:, o:o + VD_W]; o += VD_W
    uf = p[:, o:o + UF_W]

    bd = bd_ref[...]
    qa = qa * lax.rsqrt(_segmean(qa * qa, bd) + EPS) * gq_ref[...]
    ka = ka * lax.rsqrt(_segmean(ka * ka, bd[:KA_W, :KA_W]) + EPS) * gk_ref[...]
    if caches:
        cka_ref[...] = ka
        cva_ref[...] = va
        ckd_ref[...] = kd
        cvd_ref[...] = vd
    if rope:
        qa = _rope(qa, ca_ref[...], saa_ref[...], sba_ref[...], HEAD_DIM // 4)
        ka = _rope(ka, ca_ref[:, :KA_W], saa_ref[:, :KA_W], sba_ref[:, :KA_W], HEAD_DIM // 4)
        qd = _rope(qd, cd_ref[...], sad_ref[...], sbd_ref[...], DIFF_QK_DIM // 4)
        kd = _rope(kd, cd_ref[...], sad_ref[...], sbd_ref[...], DIFF_QK_DIM // 4)
    qa_ref[...] = (qa * (HEAD_DIM ** -0.5)).astype(BF16)
    ka_ref[...] = ka.astype(BF16)
    kas_ref[...] = pltpu.roll(ka, HEAD_DIM, 1).astype(BF16)
    va_ref[...] = va.astype(BF16)
    vas_ref[...] = pltpu.roll(va, HEAD_DIM, 1).astype(BF16)
    qd_ref[...] = (qd * (DIFF_QK_DIM ** -0.5)).astype(BF16)
    kd_ref[...] = kd.astype(BF16)
    vd_ref[...] = vd.astype(BF16)
    uf_ref[...] = uf.astype(BF16)


def _pre_attention(layer, x_half, mod5, g1, w_in_r, gq, gk, bd64, rope_tabs, *, sample):
    n_rows = N_SAMPLE if sample else N_PROMPT
    n_tiles = n_rows // TM
    if sample:
        row_fn = lambda i: 1 + (i * TM) // DEC_SEQ
    else:
        row_fn = lambda i: 0
    tile = lambda w: pl.BlockSpec((TM, w), lambda i: (i, 0))
    const2 = lambda a: pl.BlockSpec(a.shape, lambda i: (0, 0))
    in_specs = [
        pl.BlockSpec((TM, D_MODEL), lambda i: (i, 0)),
        _mod_spec(layer, 0, row_fn),
        _mod_spec(layer, 1, row_fn),
        pl.BlockSpec((None, 1, D_MODEL), lambda i: (layer, 0, 0)),
        pl.BlockSpec((None, D_MODEL, P_W), lambda i: (layer, 0, 0)),
        pl.BlockSpec((None, 1, QA_W), lambda i: (layer, 0, 0)),
        pl.BlockSpec((None, 1, KA_W), lambda i: (layer, 0, 0)),
        const2(bd64),
    ]
    args = [x_half, mod5, mod5, g1, w_in_r, gq, gk, bd64]
    if sample:
        tpb = DEC_SEQ // TM
        for t in rope_tabs:
            in_specs.append(pl.BlockSpec((TM, t.shape[1]), lambda i: (i % tpb, 0)))
            args.append(t)
    widths = [QA_W, KA_W, KA_W, KA_W, KA_W, QD_W, QD_W, VD_W, UF_W]
    out_shape = [jax.ShapeDtypeStruct((n_rows, w), BF16) for w in widths]
    out_specs = [tile(w) for w in widths]
    if not sample:
        for w in (KA_W, KA_W, QD_W, VD_W):
            out_shape.append(jax.ShapeDtypeStruct((n_rows, w), F32))
            out_specs.append(tile(w))
    return pl.pallas_call(
        functools.partial(_pre_kernel, rope=sample, caches=not sample),
        out_shape=out_shape,
        grid=(n_tiles,),
        in_specs=in_specs,
        out_specs=out_specs,
        compiler_params=_cparams(("arbitrary",)),
        name="pre_attention_sample" if sample else "pre_attention_prompt",
    )(*args)


def _softmax_pv(qcol, k_pieces, v_pieces):
    s = [_dot_nt(qcol, k) for k in k_pieces]
    m = s[0].max(axis=-1, keepdims=True)
    for si in s[1:]:
        m = jnp.maximum(m, si.max(axis=-1, keepdims=True))
    acc = None
    den = None
    for si, v in zip(s, v_pieces):
        e = jnp.exp(si - m)
        d = e.sum(axis=-1, keepdims=True)
        o = _dot(e.astype(BF16), v)
        acc = o if acc is None else acc + o
        den = d if den is None else den + d
    return acc * (1.0 / den)


def _attn_kernel(*refs, n_pieces, lam_init):
    qa_ref, qd_ref = refs[:2]
    pos = 2
    pieces = []
    for _ in range(n_pieces):
        pieces.append(refs[pos:pos + 6])
        pos += 6
    lam_ref, subg_ref, o_ref = refs[pos:pos + 3]

    lamv = lam_ref[...]
    s1 = jnp.sum(lamv[0:1] * lamv[1:2], axis=-1, keepdims=True)
    s2 = jnp.sum(lamv[2:3] * lamv[3:4], axis=-1, keepdims=True)
    lam = jnp.exp(s1) - jnp.exp(s2) + lam_init

    def halves(x):
        lo = lax.broadcasted_iota(jnp.int32, x.shape, 1) < HEAD_DIM
        z = jnp.zeros_like(x)
        return jnp.where(lo, x, z), jnp.where(lo, z, x)

    k_var, v_var = [], []
    for (ka_ref, kas_ref, va_ref, vas_ref, _, _) in pieces:
        a_k, b_k = halves(ka_ref[...])
        c_k, d_k = halves(kas_ref[...])
        a_v, b_v = halves(va_ref[...])
        c_v, d_v = halves(vas_ref[...])
        k_var.append((a_k, d_k, a_k, b_k, c_k, b_k))
        v_var.append((a_v, d_v, a_v, b_v, c_v, b_v))
    for j in range(ATTN_Q_HEADS // 2):
        qcol = qa_ref[:, j * LANES:(j + 1) * LANES]
        col = None
        for h in (2 * j, 2 * j + 1):
            o = _softmax_pv(qcol, [kv[h] for kv in k_var], [vv[h] for vv in v_var])
            col = o if col is None else col + o
        o_ref[:, j * LANES:(j + 1) * LANES] = col.astype(BF16)

    for h in range(DIFF_HEADS):
        sl = slice(h * LANES, (h + 1) * LANES)
        qcol = qd_ref[:, sl]
        k1, k2, vs = [], [], []
        for (_, _, _, _, kd_ref, vd_ref) in pieces:
            a, b = halves(kd_ref[:, sl])
            k1.append(a)
            k2.append(b)
            vs.append(vd_ref[:, sl])
        od = _softmax_pv(qcol, k1, vs) - lam * _softmax_pv(qcol, k2, vs)
        ms = jnp.sum(od * od, axis=-1, keepdims=True) * (1.0 / DIFF_V_DIM)
        od = od * lax.rsqrt(ms + EPS) * subg_ref[:, sl] * (1.0 - lam_init)
        o_ref[:, QA_W + h * LANES:QA_W + (h + 1) * LANES] = od.astype(BF16)


def _attention(layer, pre, prefix, lam_in, subg, *, sample):
    qa, ka, kas, va, vas, qd, kd, vd = pre[:8]
    lam_init = 0.8 - 0.6 * math.exp(-0.3 * layer)
    if sample:
        nb, seq, nq = DEC_BATCH, DEC_SEQ, DEC_SEQ // TQ
    else:
        nb, seq, nq = BATCH, SEQ, SEQ // TQ
    qspec = lambda w: pl.BlockSpec((TQ, w), lambda b, q: (b * nq + q, 0))
    kvspec = lambda w: pl.BlockSpec((seq, w), lambda b, q: (b, 0))
    in_specs = [qspec(QA_W), qspec(QD_W)]
    args = [qa, qd]
    n_pieces = 1
    if sample:
        n_pieces = 2
        for a in prefix:
            in_specs.append(pl.BlockSpec((None, None, PAST_LEN, a.shape[-1]),
                                         lambda b, q: (b, layer, 0, 0)))
            args.append(a)
    for a, w in ((ka, KA_W), (kas, KA_W), (va, KA_W), (vas, KA_W), (kd, QD_W), (vd, VD_W)):
        in_specs.append(kvspec(w))
        args.append(a)
    in_specs += [pl.BlockSpec((None, 4, DIFF_QK_DIM), lambda b, q: (layer, 0, 0)),
                 pl.BlockSpec((None, 1, VD_W), lambda b, q: (layer, 0, 0))]
    args += [lam_in, subg]
    return pl.pallas_call(
        functools.partial(_attn_kernel, n_pieces=n_pieces, lam_init=lam_init),
        out_shape=jax.ShapeDtypeStruct((nb * seq, ATT_W), BF16),
        grid=(nb, nq),
        in_specs=in_specs,
        out_specs=pl.BlockSpec((TQ, ATT_W), lambda b, q: (b * nq + q, 0)),
        compiler_params=_cparams(("arbitrary", "arbitrary")),
        name="attention_sample" if sample else "attention_prompt",
    )(*args)


def _fourier_kernel(u_ref, cs_ref, tw_ref, o_ref, ab_ref, *, seq, norm):
    @pl.when(pl.program_id(1) == 0)
    def _():
        ab = _dot(u_ref[...], cs_ref[...])
        ab_ref[0:seq, :] = ab[:, :UF_W].astype(BF16)
        ab_ref[seq:2 * seq, :] = ab[:, UF_W:].astype(BF16)

    o_ref[...] = (_dot(tw_ref[...], ab_ref[...]) * norm).astype(BF16)


def _fourier(uf, cs64, tw, *, sample):
    if sample:
        nb, seq, tq = DEC_BATCH, DEC_SEQ, TQF
    else:
        nb, seq, tq = BATCH, SEQ, SEQ
    nq = seq // tq
    in_specs = [pl.BlockSpec((seq, UF_W), lambda b, q: (b, 0)),
                pl.BlockSpec(cs64.shape, lambda b, q: (0, 0)),
                pl.BlockSpec((tq, 2 * seq), lambda b, q: (q, 0))]
    return pl.pallas_call(
        functools.partial(_fourier_kernel, seq=seq, norm=(seq * FOURIER_DIM) ** -0.5),
        out_shape=jax.ShapeDtypeStruct((nb * seq, UF_W), BF16),
        grid=(nb, nq),
        in_specs=in_specs,
        out_specs=pl.BlockSpec((tq, UF_W), lambda b, q: (b * nq + q, 0)),
        scratch_shapes=[pltpu.VMEM((2 * seq, UF_W), BF16)],
        compiler_params=_cparams(("arbitrary", "arbitrary")),
        name="fourier_sample" if sample else "fourier_prompt",
    )(uf, cs64, tw)


def _post_kernel(xp_ref, xs_ref, attp_ref, atts_ref, fop_ref, fos_ref, woa_ref, wof_ref,
                 gt1_ref, sh2_ref, sc2_ref, g2_ref, wrh_ref, wrl_ref, br_ref, ltri_ref,
                 xn_ref, h2_ref, route_ref, cnt_ref, carry_ref):
    @pl.when(pl.program_id(0) == 0)
    def _():
        carry_ref[...] = jnp.zeros_like(carry_ref)

    is_prompt = pl.program_id(0) < N_PROMPT // TM
    pick = lambda p_ref, s_ref: jnp.where(is_prompt, p_ref[...], s_ref[...])
    mix = (_dot(pick(attp_ref, atts_ref), woa_ref[...])
           + _dot(pick(fop_ref, fos_ref), wof_ref[...]))
    xn = pick(xp_ref, xs_ref) + gt1_ref[...] * mix
    xn_ref[...] = xn
    h2 = _rms(xn, g2_ref[...]) * (1.0 + sc2_ref[...]) + sh2_ref[...]
    h2_ref[...] = h2

    h_hi, h_lo = _split(h2)
    wrh = wrh_ref[...]
    logits = _dot(h_hi, wrh) + _dot(h_hi, wrl_ref[...]) + _dot(h_lo, wrh) + br_ref[...]
    lane = lax.broadcasted_iota(jnp.int32, logits.shape, 1).astype(F32)
    big = float(LANES)

    def first_max(v):
        m = v.max(axis=-1, keepdims=True)
        idx = jnp.where(v == m, lane, big).min(axis=-1, keepdims=True)
        return m, idx

    lg = jnp.where(lane < N_GROUPS, logits, NEG)
    mg, gi = first_max(lg)
    pg_top = 1.0 / jnp.exp(lg - mg).sum(axis=-1, keepdims=True)
    e_lo = N_GROUPS + gi * EXPERTS_PER_GROUP
    le = jnp.where((lane >= e_lo) & (lane < e_lo + EXPERTS_PER_GROUP), logits, NEG)
    m0, i0 = first_max(le)
    m1, i1 = first_max(jnp.where(lane == i0, NEG, le))
    e1 = jnp.exp(m1 - m0)
    inv = 1.0 / (1.0 + e1)
    w0 = pg_top * inv
    w1 = pg_top * (e1 * inv)

    hot0 = lane == i0
    hot1 = lane == i1
    onehot = jnp.where(hot0 | hot1, 1.0, 0.0)
    before = _dot(ltri_ref[...], onehot.astype(BF16)) + carry_ref[0:1, :]
    r0 = jnp.where(hot0, before, 0.0).sum(axis=-1, keepdims=True)
    r1 = jnp.where(hot1, before, 0.0).sum(axis=-1, keepdims=True)
    carry = carry_ref[...] + onehot.sum(axis=0, keepdims=True)
    carry_ref[...] = carry
    cnt_ref[...] = carry

    route = jnp.where(lane == 0, w0, 0.0)
    route = jnp.where(lane == 1, w1, route)
    route = jnp.where(lane == 2, i0 - N_GROUPS, route)
    route = jnp.where(lane == 3, i1 - N_GROUPS, route)
    route = jnp.where(lane == 4, r0, route)
    route = jnp.where(lane == 5, r1, route)
    route_ref[...] = route


def _post_attention(layer, x_pair, att_pair, fo_pair, woa, wof, mod5, g2, wr_hi, wr_lo, br, ltri):
    n_p = N_PROMPT // TM
    tile = lambda w: pl.BlockSpec((TM, w), lambda i: (i, 0))
    ptile = lambda w: pl.BlockSpec((TM, w), lambda i: (jnp.minimum(i, n_p - 1), 0))
    stile = lambda w: pl.BlockSpec((TM, w), lambda i: (jnp.maximum(i - n_p, 0), 0))
    lay3 = lambda a: pl.BlockSpec((None,) + a.shape[1:], lambda i: (layer, 0, 0))
    return pl.pallas_call(
        _post_kernel,
        out_shape=[jax.ShapeDtypeStruct((N_TOK, D_MODEL), F32),
                   jax.ShapeDtypeStruct((N_TOK, D_MODEL), F32),
                   jax.ShapeDtypeStruct((N_TOK, LANES), F32),
                   jax.ShapeDtypeStruct((SUBLANES, LANES), F32)],
        grid=(N_TOK // TM,),
        in_specs=[ptile(D_MODEL), stile(D_MODEL), ptile(ATT_W), stile(ATT_W), ptile(UF_W), stile(UF_W),
                  lay3(woa), lay3(wof),
                  _mod_spec(layer, 2, _merged_row), _mod_spec(layer, 3, _merged_row),
                  _mod_spec(layer, 4, _merged_row), lay3(g2), lay3(wr_hi), lay3(wr_lo), lay3(br),
                  pl.BlockSpec(ltri.shape, lambda i: (0, 0))],
        out_specs=[tile(D_MODEL), tile(D_MODEL), tile(LANES),
                   pl.BlockSpec((SUBLANES, LANES), lambda i: (0, 0))],
        scratch_shapes=[pltpu.VMEM((SUBLANES, LANES), F32)],
        compiler_params=_cparams(("arbitrary",)),
        name="post_attention_router",
    )(*x_pair, *att_pair, *fo_pair, woa, wof, mod5, mod5, mod5, g2, wr_hi, wr_lo, br, ltri)


def _row_copy(src_hbm, dst_hbm, src_row, dst_row, sem):
    return pltpu.make_async_copy(src_hbm.at[pl.ds(src_row, 1)], dst_hbm.at[pl.ds(dst_row, 1)], sem)


def _sort_plan(route, counts):
    cnt = counts[0, N_GROUPS:N_GROUPS + N_EXPERTS].astype(jnp.int32)
    padded = ((cnt + TMM - 1) // TMM) * TMM
    ends = jnp.cumsum(padded)
    starts = ends - padded
    ids = jnp.arange(N_EXPERTS, dtype=jnp.int32)
    expert = route[:, 2:4].astype(jnp.int32)
    rank = route[:, 4:6].astype(jnp.int32)
    start_of = jnp.sum(jnp.where(expert[..., None] == ids, starts, 0), axis=-1)
    pos = (start_of + rank).reshape(-1)
    tile_start = jnp.arange(N_MTILES, dtype=jnp.int32) * TMM
    n_active = ends[-1] // TMM
    tile = jnp.minimum(jnp.arange(N_MTILES, dtype=jnp.int32), jnp.maximum(n_active - 1, 0))
    te = jnp.sum((ends[None, :] <= (tile * TMM)[:, None]).astype(jnp.int32), axis=1)
    te = jnp.minimum(te, N_EXPERTS - 1)
    active = (tile_start < ends[-1]).astype(jnp.int32)
    is_last = jnp.sum(((ends[None, :] - TMM) == tile_start[:, None]) & (padded[None, :] > 0), axis=1)
    zero_tile = ((is_last > 0) | (active == 0)).astype(jnp.int32)
    return pos, te, tile, active, zero_tile


def _dispatch_kernel(zero_ref, pos_ref, h2_hbm, xs_hbm, zbuf_ref, sem):
    i = pl.program_id(0)

    @pl.when(i == 0)
    def _():
        zbuf_ref[...] = jnp.zeros_like(zbuf_ref)
        tile_copy = lambda j: pltpu.make_async_copy(zbuf_ref, xs_hbm.at[pl.ds(j * TMM, TMM)], sem.at[1])

        def zstart(j, c):
            @pl.when(zero_ref[j] == 1)
            def _():
                tile_copy(j).start()
            return c

        def zwait(j, c):
            @pl.when(zero_ref[j] == 1)
            def _():
                tile_copy(j).wait()
            return c

        lax.fori_loop(0, N_MTILES, zstart, 0)
        lax.fori_loop(0, N_MTILES, zwait, 0)

    def start(a, c):
        tok = i * TD + lax.shift_right_logical(a, 1)
        _row_copy(h2_hbm, xs_hbm, tok, pos_ref[0, 0, a], sem.at[0]).start()
        return c

    def wait(a, c):
        _row_copy(h2_hbm, xs_hbm, 0, 0, sem.at[0]).wait()
        return c

    lax.fori_loop(0, 2 * TD, start, 0, unroll=8)
    lax.fori_loop(0, 2 * TD, wait, 0, unroll=8)


def _dispatch(zero_tile, pos, h2):
    grid_spec = pltpu.PrefetchScalarGridSpec(
        num_scalar_prefetch=1,
        grid=(N_TOK // TD,),
        in_specs=[pl.BlockSpec((1, 1, 2 * TD), lambda i, z: (i, 0, 0), memory_space=pltpu.SMEM),
                  pl.BlockSpec(memory_space=pl.ANY)],
        out_specs=pl.BlockSpec(memory_space=pl.ANY),
        scratch_shapes=[pltpu.VMEM((TMM, D_MODEL), F32), pltpu.SemaphoreType.DMA((2,))],
    )
    return pl.pallas_call(
        _dispatch_kernel,
        out_shape=jax.ShapeDtypeStruct((P_ROWS, D_MODEL), F32),
        grid_spec=grid_spec,
        compiler_params=_cparams(("arbitrary",)),
        name="moe_dispatch",
    )(zero_tile, pos.reshape(N_TOK // TD, 1, 2 * TD), h2)


def _moe_kernel(te_ref, tile_ref, act_ref, xs_ref, wg_ref, wu_ref, wd_ref, ys_ref,
                wgb_ref, wub_ref, wdb_ref):
    j = pl.program_id(0)
    prev = te_ref[jnp.maximum(j - 1, 0)]

    @pl.when((j == 0) | (te_ref[j] != prev))
    def _():
        wgb_ref[...] = wg_ref[...].astype(BF16)
        wub_ref[...] = wu_ref[...].astype(BF16)
        wdb_ref[...] = wd_ref[...].astype(BF16)

    @pl.when(act_ref[j] == 1)
    def _():
        x = xs_ref[...].astype(BF16)
        hg = _dot(x, wgb_ref[...])
        hu = _dot(x, wub_ref[...])
        a = (hg / (1.0 + jnp.exp(-hg))) * hu
        ys_ref[...] = _dot(a.astype(BF16), wdb_ref[...])

    @pl.when(act_ref[j] == 0)
    def _():
        ys_ref[...] = jnp.zeros_like(ys_ref)


def _moe(layer, te, tile, active, xs, w_gate, w_up, w_down):
    wspec = lambda shape: pl.BlockSpec((None, None) + shape, lambda j, te, tl, act: (layer, te[j], 0, 0))
    grid_spec = pltpu.PrefetchScalarGridSpec(
        num_scalar_prefetch=3,
        grid=(N_MTILES,),
        in_specs=[
            pl.BlockSpec((TMM, D_MODEL), lambda j, te, tl, act: (tl[j], 0)),
            wspec((D_MODEL, D_EXPERT)), wspec((D_MODEL, D_EXPERT)), wspec((D_EXPERT, D_MODEL)),
        ],
        out_specs=pl.BlockSpec((TMM, D_MODEL), lambda j, te, tl, act: (j, 0)),
        scratch_shapes=[pltpu.VMEM((D_MODEL, D_EXPERT), BF16), pltpu.VMEM((D_MODEL, D_EXPERT), BF16),
                        pltpu.VMEM((D_EXPERT, D_MODEL), BF16)],
    )
    return pl.pallas_call(
        _moe_kernel,
        out_shape=jax.ShapeDtypeStruct((P_ROWS, D_MODEL), F32),
        grid_spec=grid_spec,
        compiler_params=_cparams(("arbitrary",)),
        name="moe_grouped_matmul",
    )(te, tile, active, xs, w_gate, w_up, w_down)


def _combine_kernel(pos_ref, ys_hbm, xn_ref, route_ref, gt2_ref, *rest, final):
    o_ref, ybuf_ref, sem = rest[-3:]

    def row(a, src):
        return _row_copy(ys_hbm, ybuf_ref.at[a & 1], src, lax.shift_right_logical(a, 1), sem.at[0])

    def start(a, c):
        row(a, pos_ref[0, 0, a]).start()
        return c

    def wait(a, c):
        row(a, 0).wait()
        return c

    lax.fori_loop(0, 2 * TM, start, 0, unroll=8)
    lax.fori_loop(0, 2 * TM, wait, 0, unroll=8)

    r = route_ref[...]
    y = r[:, 0:1] * ybuf_ref[0] + r[:, 1:2] * ybuf_ref[1]
    x = xn_ref[...] + gt2_ref[...] * y
    if final:
        x = _rms(x, rest[0][...])
    o_ref[...] = x


def _combine(layer, xn, ys, pos, route, mod5, final_g, *, sample):
    final = final_g is not None
    if sample:
        n_rows, off = N_SAMPLE, N_PROMPT // TM
        row_fn = lambda i: 1 + (i * TM) // DEC_SEQ
    else:
        n_rows, off, row_fn = N_PROMPT, 0, (lambda i: 0)
    tile = lambda w: pl.BlockSpec((TM, w), lambda i: (i + off, 0))
    in_specs = [pl.BlockSpec((1, 1, 2 * TM), lambda i: (i + off, 0, 0), memory_space=pltpu.SMEM),
                pl.BlockSpec(memory_space=pl.ANY),
                tile(D_MODEL), tile(LANES), _mod_spec(layer, 5, row_fn)]
    args = [pos.reshape(N_TOK // TM, 1, 2 * TM), ys, xn, route, mod5]
    if final:
        in_specs.append(pl.BlockSpec((1, D_MODEL), lambda i: (0, 0)))
        args.append(final_g)
    return pl.pallas_call(
        functools.partial(_combine_kernel, final=final),
        out_shape=jax.ShapeDtypeStruct((n_rows, D_MODEL), F32),
        grid=(n_rows // TM,),
        in_specs=in_specs,
        out_specs=pl.BlockSpec((TM, D_MODEL), lambda i: (i, 0)),
        scratch_shapes=[pltpu.VMEM((2, TM, D_MODEL), F32), pltpu.SemaphoreType.DMA((1,))],
        compiler_params=_cparams(("arbitrary",)),
        name="moe_combine_" + ("sample" if sample else "prompt") + ("_final" if final else ""),
    )(*args)


def _pad_segments(w, axis, nseg, seg, segp):
    shape = w.shape
    w = w.reshape(shape[:axis] + (nseg, seg) + shape[axis + 1:])
    pad = [(0, 0)] * w.ndim
    pad[axis + 1] = (0, segp - seg)
    w = jnp.pad(w, pad)
    return w.reshape(shape[:axis] + (nseg * segp,) + shape[axis + 1:])


def _axial_angles(seq, dim):
    rows = seq // GRID_W
    r = jnp.repeat(jnp.arange(rows), GRID_W).astype(F32)
    col = jnp.tile(jnp.arange(GRID_W), rows).astype(F32)
    n = dim // 4
    freqs = ROPE_THETA ** (-jnp.arange(n, dtype=F32) / n)
    return r[:, None] * freqs, col[:, None] * freqs


def _rope_tables(seq, dim, slot, nslots):
    ang_r, ang_c = _axial_angles(seq, dim)
    n = dim // 4
    z = jnp.zeros((seq, n), F32)
    pad = jnp.zeros((seq, slot - dim), F32)
    cr, sr, cc, sc = jnp.cos(ang_r), jnp.sin(ang_r), jnp.cos(ang_c), jnp.sin(ang_c)
    c = jnp.concatenate([cr, cr, cc, cc, pad], axis=1)
    sa = jnp.concatenate([-sr, z, -sc, z, pad], axis=1)
    sb = jnp.concatenate([z, sr, z, sc, pad], axis=1)
    return [jnp.tile(t, (1, nslots)) for t in (c, sa, sb)]


def _dft_tables(seq):
    t = jnp.arange(seq, dtype=jnp.int32)
    ang = ((t[:, None] * t[None, :]) % seq).astype(F32) * (2.0 * math.pi / seq)
    return jnp.concatenate([jnp.cos(ang), -jnp.sin(ang)], axis=1).astype(BF16)


def _channel_dft():
    c = jnp.arange(FOURIER_DIM, dtype=jnp.int32)
    ang = ((c[:, None] * c[None, :]) % FOURIER_DIM).astype(F32) * (2.0 * math.pi / FOURIER_DIM)
    eye = jnp.eye(FOURIER_GROUPS, dtype=F32)
    return jnp.concatenate([jnp.kron(eye, jnp.cos(ang)), jnp.kron(eye, jnp.sin(ang))],
                           axis=1).astype(BF16)


def kernel(x_prompt, x_sample, cache_attn_k, cache_attn_v, cache_diff_k, cache_diff_v, c, c_ctx,
           norm1_g, w_mod, b_mod, w_in, w_out, q_norm_g, k_norm_g,
           lambda_q1, lambda_k1, lambda_q2, lambda_k2, subln_g, norm2_g,
           w_grp, b_grp, w_exp, b_exp, w_gate, w_up, w_down, final_g):
    cv = jnp.concatenate([c_ctx[None, :], c, jnp.zeros((MOD_ROWS - 1 - DEC_BATCH, D_MODEL), F32)], axis=0)
    mod5 = _modulation(cv, w_mod, b_mod).reshape(DEPTH, MOD_ROWS, N_MOD, 1, D_MODEL)

    o0, o1, o2, o3, o4 = QA_W + 2 * KA_W, QA_W + 2 * KA_W + 384, QA_W + 2 * KA_W + 768, 1792, 2048
    w_in_r = jnp.concatenate([
        w_in[..., :o0],
        _pad_segments(w_in[..., o0:o1], 2, 2 * DIFF_HEADS, DIFF_QK_DIM, DQK_SLOT),
        _pad_segments(w_in[..., o1:o2], 2, 2 * DIFF_HEADS, DIFF_QK_DIM, DQK_SLOT),
        _pad_segments(w_in[..., o2:o3], 2, DIFF_HEADS, DIFF_V_DIM, LANES),
        w_in[..., o3:o4]], axis=-1).astype(BF16)
    woa = jnp.concatenate([w_out[:, :QA_W],
                           _pad_segments(w_out[:, QA_W:QA_W + DIFF_HEADS * DIFF_V_DIM], 1,
                                         DIFF_HEADS, DIFF_V_DIM, LANES)], axis=1).astype(BF16)
    wof = w_out[:, QA_W + DIFF_HEADS * DIFF_V_DIM:].astype(BF16)
    g1 = norm1_g.reshape(DEPTH, 1, D_MODEL)
    g2 = norm2_g.reshape(DEPTH, 1, D_MODEL)
    gq = jnp.tile(q_norm_g, (1, ATTN_Q_HEADS)).reshape(DEPTH, 1, QA_W)
    gk = jnp.tile(k_norm_g, (1, ATTN_KV_HEADS)).reshape(DEPTH, 1, KA_W)
    subg = _pad_segments(jnp.tile(subln_g, (1, DIFF_HEADS)), 1, DIFF_HEADS, DIFF_V_DIM, LANES
                         ).reshape(DEPTH, 1, VD_W)
    lam_in = jnp.stack([lambda_q1, lambda_k1, lambda_q2, lambda_k2], axis=1)
    w_r = jnp.concatenate([w_grp, w_exp, jnp.zeros((DEPTH, D_MODEL, LANES - N_ROUTE), F32)], axis=-1)
    wr_hi = w_r.astype(BF16)
    wr_lo = (w_r - wr_hi.astype(F32)).astype(BF16)
    br = jnp.concatenate([b_grp, b_exp, jnp.zeros((DEPTH, LANES - N_ROUTE), F32)], axis=-1
                         ).reshape(DEPTH, 1, LANES)
    seg = np.arange(QA_W) // HEAD_DIM
    bd64 = jnp.asarray((seg[:, None] == seg[None, :]).astype(np.float32) / HEAD_DIM, dtype=BF16)
    ltri = jnp.asarray(np.tril(np.ones((TM, TM), np.float32), -1), dtype=BF16)
    rope_tabs = (_rope_tables(DEC_SEQ, HEAD_DIM, HEAD_DIM, ATTN_Q_HEADS)
                 + _rope_tables(DEC_SEQ, DIFF_QK_DIM, DQK_SLOT, 2 * DIFF_HEADS))
    cs64 = _channel_dft()
    tw_p = _dft_tables(SEQ)
    tw_s = _dft_tables(DEC_SEQ)
    pka = cache_attn_k.reshape(DEC_BATCH, DEPTH, PAST_LEN, KA_W)
    pva = cache_attn_v.reshape(DEC_BATCH, DEPTH, PAST_LEN, KA_W)
    pkd = _pad_segments(cache_diff_k.reshape(DEC_BATCH, DEPTH, PAST_LEN, 2 * DIFF_HEADS * DIFF_QK_DIM),
                        3, 2 * DIFF_HEADS, DIFF_QK_DIM, DQK_SLOT)
    pvd = _pad_segments(cache_diff_v.reshape(DEC_BATCH, DEPTH, PAST_LEN, DIFF_HEADS * DIFF_V_DIM),
                        3, DIFF_HEADS, DIFF_V_DIM, LANES)
    prefix = [pka.astype(BF16), jnp.roll(pka, HEAD_DIM, axis=-1).astype(BF16),
              pva.astype(BF16), jnp.roll(pva, HEAD_DIM, axis=-1).astype(BF16),
              pkd.astype(BF16), pvd.astype(BF16)]

    caches = []
    x_p = x_prompt.reshape(N_PROMPT, D_MODEL)
    x_s = x_sample.reshape(N_SAMPLE, D_MODEL)
    for layer in range(DEPTH):
        pre_p = _pre_attention(layer, x_p, mod5, g1, w_in_r, gq, gk, bd64, None, sample=False)
        pre_s = _pre_attention(layer, x_s, mod5, g1, w_in_r, gq, gk, bd64, rope_tabs, sample=True)
        caches.append(pre_p[9:])
        att_p = _attention(layer, pre_p, None, lam_in, subg, sample=False)
        att_s = _attention(layer, pre_s, prefix, lam_in, subg, sample=True)
        fo_p = _fourier(pre_p[8], cs64, tw_p, sample=False)
        fo_s = _fourier(pre_s[8], cs64, tw_s, sample=True)
        xn, h2, route, counts = _post_attention(layer, (x_p, x_s), (att_p, att_s), (fo_p, fo_s),
                                                woa, wof, mod5, g2, wr_hi, wr_lo, br, ltri)
        pos, te, tile, active, zero_tile = _sort_plan(route, counts)
        xs = _dispatch(zero_tile, pos, h2)
        ys = _moe(layer, te, tile, active, xs, w_gate, w_up, w_down)
        fg = final_g.reshape(1, D_MODEL) if layer + 1 == DEPTH else None
        x_p = _combine(layer, xn, ys, pos, route, mod5, fg, sample=False)
        x_s = _combine(layer, xn, ys, pos, route, mod5, fg, sample=True)
    y_prompt, y_sample = x_p, x_s

    def stack(i, shape, keep):
        arrs = [cl[i].reshape(shape)[..., :keep] for cl in caches]
        return jnp.stack(arrs, axis=1)

    new_attn_k = stack(0, (BATCH, SEQ, ATTN_KV_HEADS, HEAD_DIM), HEAD_DIM)
    new_attn_v = stack(1, (BATCH, SEQ, ATTN_KV_HEADS, HEAD_DIM), HEAD_DIM)
    new_diff_k = stack(2, (BATCH, SEQ, DIFF_HEADS, 2, DQK_SLOT), DIFF_QK_DIM)
    new_diff_v = stack(3, (BATCH, SEQ, DIFF_HEADS, LANES), DIFF_V_DIM)
    return (y_prompt.reshape(BATCH, SEQ, D_MODEL), y_sample.reshape(DEC_BATCH, DEC_SEQ, D_MODEL),
            new_attn_k, new_attn_v, new_diff_k, new_diff_v)
```

```python
import functools
import math

import jax
import jax.numpy as jnp
import numpy as np
from jax import lax
from jax.experimental import pallas as pl
from jax.experimental.pallas import tpu as pltpu

F32 = jnp.float32
BF16 = jnp.bfloat16

D_MODEL = 1024
BATCH = 32
SEQ = 256
DEPTH = 2
DEC_BATCH = 4
DEC_SEQ = 2048
PAST_LEN = 256
GRID_W = 64
HEAD_DIM = 64
ATTN_Q_HEADS = 6
ATTN_KV_HEADS = 2
DIFF_HEADS = 4
DIFF_QK_DIM = 48
DIFF_V_DIM = 96
FOURIER_GROUPS = 4
FOURIER_DIM = 64
N_GROUPS = 4
EXPERTS_PER_GROUP = 8
N_EXPERTS = N_GROUPS * EXPERTS_PER_GROUP
D_EXPERT = 512
ROPE_THETA = 10000.0
EPS = 1e-6
LOG2E = math.log2(math.e)
N_MOD = 6

LANES = 128
SUBLANES = 8

N_PROMPT = BATCH * SEQ
N_SAMPLE = DEC_BATCH * DEC_SEQ
N_TOK = N_PROMPT + N_SAMPLE
QA_W = ATTN_Q_HEADS * HEAD_DIM
KA_W = ATTN_KV_HEADS * HEAD_DIM
DQK_SLOT = 64
QD_W = DIFF_HEADS * 2 * DQK_SLOT
VD_W = DIFF_HEADS * LANES
UF_W = FOURIER_GROUPS * FOURIER_DIM
P_W = QA_W + 2 * KA_W + 2 * QD_W + VD_W + UF_W
ATT_W = QA_W + VD_W
MOD_ROWS = 8
N_ROUTE = N_GROUPS + N_EXPERTS

TM = 512
TQ_SAMPLE = 256
TQF = 512
TMM = 256
N_ASSIGN = 2 * N_TOK
P_ROWS = N_ASSIGN + N_EXPERTS * TMM
N_MTILES = P_ROWS // TMM
VMEM_LIMIT = 48 * 1024 * 1024
NEG = -1e30


def _cparams(sem):
    return pltpu.CompilerParams(dimension_semantics=sem, vmem_limit_bytes=VMEM_LIMIT)


def _dot(a, b):
    return jnp.dot(a, b, preferred_element_type=F32)


def _dot_nt(a, b):
    return lax.dot_general(a, b, (((1,), (1,)), ((), ())), preferred_element_type=F32)


def _split(x):
    hi = x.astype(BF16)
    lo = (x - hi.astype(F32)).astype(BF16)
    return hi, lo


def _rms(x, g):
    ms = jnp.mean(x * x, axis=-1, keepdims=True)
    return x * lax.rsqrt(ms + EPS) * g


def _mod_kernel(cv_ref, w_ref, b_ref, o_ref):
    cv = cv_ref[...]
    s = cv / (1.0 + jnp.exp(-cv))
    s_hi, s_lo = _split(s)
    w_hi, w_lo = _split(w_ref[...])
    o_ref[...] = _dot(s_hi, w_hi) + _dot(s_hi, w_lo) + _dot(s_lo, w_hi) + b_ref[...]


def _modulation(cv, w_mod, b_mod):
    tn = 1536
    return pl.pallas_call(
        _mod_kernel,
        out_shape=jax.ShapeDtypeStruct((DEPTH, MOD_ROWS, N_MOD * D_MODEL), F32),
        grid=(DEPTH, N_MOD * D_MODEL // tn),
        in_specs=[
            pl.BlockSpec((MOD_ROWS, D_MODEL), lambda l, n: (0, 0)),
            pl.BlockSpec((None, D_MODEL, tn), lambda l, n: (l, 0, n)),
            pl.BlockSpec((None, 1, tn), lambda l, n: (l, 0, n)),
        ],
        out_specs=pl.BlockSpec((None, MOD_ROWS, tn), lambda l, n: (l, 0, n)),
        compiler_params=_cparams(("arbitrary", "arbitrary")),
        name="modulation",
    )(cv, w_mod, b_mod.reshape(DEPTH, 1, N_MOD * D_MODEL))


def _mod_spec(layer, chunk, row_fn):
    return pl.BlockSpec((None, None, None, 1, D_MODEL),
                        lambda i: (layer, row_fn(i), chunk, 0, 0))


def _merged_row(i):
    n_p = N_PROMPT // TM
    return jnp.where(i < n_p, 0, 1 + ((i - n_p) * TM) // DEC_SEQ)


def _segmean(x2, bd):
    hi, lo = _split(x2)
    return _dot(hi, bd) + _dot(lo, bd)


def _rope(x, c, sa, sb, n):
    w = x.shape[1]
    return x * c + pltpu.roll(x, w - n, 1) * sa + pltpu.roll(x, n, 1) * sb


def _pre_kernel(*refs, rope, caches):
    x_ref, sh_ref, sc_ref, g1_ref, w_ref, gq_ref, gk_ref, bd_ref = refs[:8]
    pos = 8
    if rope:
        ca_ref, saa_ref, sba_ref, cd_ref, sad_ref, sbd_ref = refs[pos:pos + 6]
        pos += 6
    (qa_ref, ka_ref, kas_ref, va_ref, vas_ref, qd_ref, kd_ref, vd_ref, uf_ref) = refs[pos:pos + 9]
    pos += 9
    if caches:
        cka_ref, cva_ref, ckd_ref, cvd_ref = refs[pos:pos + 4]

    x = x_ref[...]
    h = _rms(x, g1_ref[...]) * (1.0 + sc_ref[...]) + sh_ref[...]
    p = _dot(h.astype(BF16), w_ref[...])
    o = 0
    qa = p[:, o:o + QA_W]; o += QA_W
    ka = p[:, o:o + KA_W]; o += KA_W
    va = p[:, o:o + KA_W]; o += KA_W
    qd = p[:, o:o + QD_W]; o += QD_W
    kd = p[:, o:o + QD_W]; o += QD_W
    vd = p[:, o:o + VD_W]; o += VD_W
    uf = p[:, o:o + UF_W]

    bd = bd_ref[...]
    qa = qa * lax.rsqrt(_segmean(qa * qa, bd) + EPS) * gq_ref[...]
    ka = ka * lax.rsqrt(_segmean(ka * ka, bd[:KA_W, :KA_W]) + EPS) * gk_ref[...]
    if caches:
        cka_ref[...] = ka
        cva_ref[...] = va
        ckd_ref[...] = kd
        cvd_ref[...] = vd
    if rope:
        qa = _rope(qa, ca_ref[...], saa_ref[...], sba_ref[...], HEAD_DIM // 4)
        ka = _rope(ka, ca_ref[:, :KA_W], saa_ref[:, :KA_W], sba_ref[:, :KA_W], HEAD_DIM // 4)
        qd = _rope(qd, cd_ref[...], sad_ref[...], sbd_ref[...], DIFF_QK_DIM // 4)
        kd = _rope(kd, cd_ref[...], sad_ref[...], sbd_ref[...], DIFF_QK_DIM // 4)
    qa_ref[...] = (qa * (HEAD_DIM ** -0.5 * LOG2E)).astype(BF16)
    ka_ref[...] = ka.astype(BF16)
    kas_ref[...] = pltpu.roll(ka, HEAD_DIM, 1).astype(BF16)
    va_ref[...] = va.astype(BF16)
    vas_ref[...] = pltpu.roll(va, HEAD_DIM, 1).astype(BF16)
    qd_ref[...] = (qd * (DIFF_QK_DIM ** -0.5 * LOG2E)).astype(BF16)
    kd_ref[...] = kd.astype(BF16)
    vd_ref[...] = vd.astype(BF16)
    uf_ref[...] = uf.astype(BF16)


def _pre_attention(layer, x_half, mod5, g1, w_in_r, gq, gk, bd64, rope_tabs, *, sample):
    n_rows = N_SAMPLE if sample else N_PROMPT
    n_tiles = n_rows // TM
    if sample:
        row_fn = lambda i: 1 + (i * TM) // DEC_SEQ
    else:
        row_fn = lambda i: 0
    tile = lambda w: pl.BlockSpec((TM, w), lambda i: (i, 0))
    const2 = lambda a: pl.BlockSpec(a.shape, lambda i: (0, 0))
    in_specs = [
        pl.BlockSpec((TM, D_MODEL), lambda i: (i, 0)),
        _mod_spec(layer, 0, row_fn),
        _mod_spec(layer, 1, row_fn),
        pl.BlockSpec((None, 1, D_MODEL), lambda i: (layer, 0, 0)),
        pl.BlockSpec((None, D_MODEL, P_W), lambda i: (layer, 0, 0)),
        pl.BlockSpec((None, 1, QA_W), lambda i: (layer, 0, 0)),
        pl.BlockSpec((None, 1, KA_W), lambda i: (layer, 0, 0)),
        const2(bd64),
    ]
    args = [x_half, mod5, mod5, g1, w_in_r, gq, gk, bd64]
    if sample:
        tpb = DEC_SEQ // TM
        for t in rope_tabs:
            in_specs.append(pl.BlockSpec((TM, t.shape[1]), lambda i: (i % tpb, 0)))
            args.append(t)
    widths = [QA_W, KA_W, KA_W, KA_W, KA_W, QD_W, QD_W, VD_W, UF_W]
    out_shape = [jax.ShapeDtypeStruct((n_rows, w), BF16) for w in widths]
    out_specs = [tile(w) for w in widths]
    if not sample:
        for w in (KA_W, KA_W, QD_W, VD_W):
            out_shape.append(jax.ShapeDtypeStruct((n_rows, w), F32))
            out_specs.append(tile(w))
    return pl.pallas_call(
        functools.partial(_pre_kernel, rope=sample, caches=not sample),
        out_shape=out_shape,
        grid=(n_tiles,),
        in_specs=in_specs,
        out_specs=out_specs,
        compiler_params=_cparams(("arbitrary",)),
        name="pre_attention_sample" if sample else "pre_attention_prompt",
    )(*args)


def _softmax_pv(qcol, k_pieces, v_pieces):
    s = [_dot_nt(qcol, k) for k in k_pieces]
    m = s[0].max(axis=-1, keepdims=True)
    for si in s[1:]:
        m = jnp.maximum(m, si.max(axis=-1, keepdims=True))
    acc = None
    den = None
    for si, v in zip(s, v_pieces):
        e = jnp.exp2(si - m)
        d = e.sum(axis=-1, keepdims=True)
        o = _dot(e.astype(BF16), v)
        acc = o if acc is None else acc + o
        den = d if den is None else den + d
    return acc * (1.0 / den)


def _attn_kernel(*refs, n_pieces, lam_init):
    qa_ref, qd_ref = refs[:2]
    pos = 2
    pieces = []
    for _ in range(n_pieces):
        pieces.append(refs[pos:pos + 6])
        pos += 6
    lam_ref, subg_ref, o_ref = refs[pos:pos + 3]

    lamv = lam_ref[...]
    s1 = jnp.sum(lamv[0:1] * lamv[1:2], axis=-1, keepdims=True)
    s2 = jnp.sum(lamv[2:3] * lamv[3:4], axis=-1, keepdims=True)
    lam = jnp.exp(s1) - jnp.exp(s2) + lam_init

    def halves(x):
        lo = lax.broadcasted_iota(jnp.int32, x.shape, 1) < HEAD_DIM
        z = jnp.zeros_like(x)
        return jnp.where(lo, x, z), jnp.where(lo, z, x)

    k_var, v_var = [], []
    for (ka_ref, kas_ref, va_ref, vas_ref, _, _) in pieces:
        a_k, b_k = halves(ka_ref[...])
        c_k, d_k = halves(kas_ref[...])
        a_v, b_v = halves(va_ref[...])
        c_v, d_v = halves(vas_ref[...])
        k_var.append((a_k, d_k, a_k, b_k, c_k, b_k))
        v_var.append((a_v, d_v, a_v, b_v, c_v, b_v))
    for j in range(ATTN_Q_HEADS // 2):
        qcol = qa_ref[:, j * LANES:(j + 1) * LANES]
        col = None
        for h in (2 * j, 2 * j + 1):
            o = _softmax_pv(qcol, [kv[h] for kv in k_var], [vv[h] for vv in v_var])
            col = o if col is None else col + o
        o_ref[:, j * LANES:(j + 1) * LANES] = col.astype(BF16)

    for h in range(DIFF_HEADS):
        sl = slice(h * LANES, (h + 1) * LANES)
        qcol = qd_ref[:, sl]
        k1, k2, vs = [], [], []
        for (_, _, _, _, kd_ref, vd_ref) in pieces:
            a, b = halves(kd_ref[:, sl])
            k1.append(a)
            k2.append(b)
            vs.append(vd_ref[:, sl])
        od = _softmax_pv(qcol, k1, vs) - lam * _softmax_pv(qcol, k2, vs)
        ms = jnp.sum(od * od, axis=-1, keepdims=True) * (1.0 / DIFF_V_DIM)
        od = od * lax.rsqrt(ms + EPS) * subg_ref[:, sl] * (1.0 - lam_init)
        o_ref[:, QA_W + h * LANES:QA_W + (h + 1) * LANES] = od.astype(BF16)


def _attention(layer, pre, prefix, lam_in, subg, *, sample):
    qa, ka, kas, va, vas, qd, kd, vd = pre[:8]
    lam_init = 0.8 - 0.6 * math.exp(-0.3 * layer)
    if sample:
        nb, seq, tq = DEC_BATCH, DEC_SEQ, TQ_SAMPLE
    else:
        nb, seq, tq = BATCH, SEQ, SEQ
    nq = seq // tq
    qspec = lambda w: pl.BlockSpec((tq, w), lambda b, q: (b * nq + q, 0))
    kvspec = lambda w: pl.BlockSpec((seq, w), lambda b, q: (b, 0))
    in_specs = [qspec(QA_W), qspec(QD_W)]
    args = [qa, qd]
    n_pieces = 1
    if sample:
        n_pieces = 2
        for a in prefix:
            in_specs.append(pl.BlockSpec((None, None, PAST_LEN, a.shape[-1]),
                                         lambda b, q: (b, layer, 0, 0)))
            args.append(a)
    for a, w in ((ka, KA_W), (kas, KA_W), (va, KA_W), (vas, KA_W), (kd, QD_W), (vd, VD_W)):
        in_specs.append(kvspec(w))
        args.append(a)
    in_specs += [pl.BlockSpec((None, 4, DIFF_QK_DIM), lambda b, q: (layer, 0, 0)),
                 pl.BlockSpec((None, 1, VD_W), lambda b, q: (layer, 0, 0))]
    args += [lam_in, subg]
    return pl.pallas_call(
        functools.partial(_attn_kernel, n_pieces=n_pieces, lam_init=lam_init),
        out_shape=jax.ShapeDtypeStruct((nb * seq, ATT_W), BF16),
        grid=(nb, nq),
        in_specs=in_specs,
        out_specs=pl.BlockSpec((tq, ATT_W), lambda b, q: (b * nq + q, 0)),
        compiler_params=_cparams(("arbitrary", "arbitrary")),
        name="attention_sample" if sample else "attention_prompt",
    )(*args)


def _fourier_kernel(u_ref, cs_ref, tw_ref, o_ref, ab_ref, *, seq, norm):
    @pl.when(pl.program_id(1) == 0)
    def _():
        ab = _dot(u_ref[...], cs_ref[...])
        ab_ref[0:seq, :] = ab[:, :UF_W].astype(BF16)
        ab_ref[seq:2 * seq, :] = ab[:, UF_W:].astype(BF16)

    o_ref[...] = (_dot(tw_ref[...], ab_ref[...]) * norm).astype(BF16)


def _fourier(uf, cs64, tw, *, sample):
    if sample:
        nb, seq, tq = DEC_BATCH, DEC_SEQ, TQF
    else:
        nb, seq, tq = BATCH, SEQ, SEQ
    nq = seq // tq
    in_specs = [pl.BlockSpec((seq, UF_W), lambda b, q: (b, 0)),
                pl.BlockSpec(cs64.shape, lambda b, q: (0, 0)),
                pl.BlockSpec((tq, 2 * seq), lambda b, q: (q, 0))]
    return pl.pallas_call(
        functools.partial(_fourier_kernel, seq=seq, norm=(seq * FOURIER_DIM) ** -0.5),
        out_shape=jax.ShapeDtypeStruct((nb * seq, UF_W), BF16),
        grid=(nb, nq),
        in_specs=in_specs,
        out_specs=pl.BlockSpec((tq, UF_W), lambda b, q: (b * nq + q, 0)),
        scratch_shapes=[pltpu.VMEM((2 * seq, UF_W), BF16)],
        compiler_params=_cparams(("arbitrary", "arbitrary")),
        name="fourier_sample" if sample else "fourier_prompt",
    )(uf, cs64, tw)


def _post_kernel(xp_ref, xs_ref, attp_ref, atts_ref, fop_ref, fos_ref, woa_ref, wof_ref,
                 gt1_ref, sh2_ref, sc2_ref, g2_ref, wrh_ref, wrl_ref, br_ref, ltri_ref,
                 xn_ref, h2_ref, route_ref, cnt_ref, carry_ref):
    @pl.when(pl.program_id(0) == 0)
    def _():
        carry_ref[...] = jnp.zeros_like(carry_ref)

    is_prompt = pl.program_id(0) < N_PROMPT // TM
    pick = lambda p_ref, s_ref: jnp.where(is_prompt, p_ref[...], s_ref[...])
    mix = (_dot(pick(attp_ref, atts_ref), woa_ref[...])
           + _dot(pick(fop_ref, fos_ref), wof_ref[...]))
    xn = pick(xp_ref, xs_ref) + gt1_ref[...] * mix
    xn_ref[...] = xn
    h2 = _rms(xn, g2_ref[...]) * (1.0 + sc2_ref[...]) + sh2_ref[...]
    h2_ref[...] = h2

    h_hi, h_lo = _split(h2)
    wrh = wrh_ref[...]
    logits = _dot(h_hi, wrh) + _dot(h_hi, wrl_ref[...]) + _dot(h_lo, wrh) + br_ref[...]
    lane = lax.broadcasted_iota(jnp.int32, logits.shape, 1).astype(F32)
    big = float(LANES)

    def first_max(v):
        m = v.max(axis=-1, keepdims=True)
        idx = jnp.where(v == m, lane, big).min(axis=-1, keepdims=True)
        return m, idx

    lg = jnp.where(lane < N_GROUPS, logits, NEG)
    mg, gi = first_max(lg)
    pg_top = 1.0 / jnp.exp(lg - mg).sum(axis=-1, keepdims=True)
    e_lo = N_GROUPS + gi * EXPERTS_PER_GROUP
    le = jnp.where((lane >= e_lo) & (lane < e_lo + EXPERTS_PER_GROUP), logits, NEG)
    m0, i0 = first_max(le)
    m1, i1 = first_max(jnp.where(lane == i0, NEG, le))
    e1 = jnp.exp(m1 - m0)
    inv = 1.0 / (1.0 + e1)
    w0 = pg_top * inv
    w1 = pg_top * (e1 * inv)

    hot0 = lane == i0
    hot1 = lane == i1
    onehot = jnp.where(hot0 | hot1, 1.0, 0.0)
    before = _dot(ltri_ref[...], onehot.astype(BF16)) + carry_ref[0:1, :]
    r0 = jnp.where(hot0, before, 0.0).sum(axis=-1, keepdims=True)
    r1 = jnp.where(hot1, before, 0.0).sum(axis=-1, keepdims=True)
    carry = carry_ref[...] + onehot.sum(axis=0, keepdims=True)
    carry_ref[...] = carry
    cnt_ref[...] = carry

    route = jnp.where(lane == 0, w0, 0.0)
    route = jnp.where(lane == 1, w1, route)
    route = jnp.where(lane == 2, i0 - N_GROUPS, route)
    route = jnp.where(lane == 3, i1 - N_GROUPS, route)
    route = jnp.where(lane == 4, r0, route)
    route = jnp.where(lane == 5, r1, route)
    route_ref[...] = route


def _post_attention(layer, x_pair, att_pair, fo_pair, woa, wof, mod5, g2, wr_hi, wr_lo, br, ltri):
    n_p = N_PROMPT // TM
    tile = lambda w: pl.BlockSpec((TM, w), lambda i: (i, 0))
    ptile = lambda w: pl.BlockSpec((TM, w), lambda i: (jnp.minimum(i, n_p - 1), 0))
    stile = lambda w: pl.BlockSpec((TM, w), lambda i: (jnp.maximum(i - n_p, 0), 0))
    lay3 = lambda a: pl.BlockSpec((None,) + a.shape[1:], lambda i: (layer, 0, 0))
    return pl.pallas_call(
        _post_kernel,
        out_shape=[jax.ShapeDtypeStruct((N_TOK, D_MODEL), F32),
                   jax.ShapeDtypeStruct((N_TOK, D_MODEL), F32),
                   jax.ShapeDtypeStruct((N_TOK, LANES), F32),
                   jax.ShapeDtypeStruct((SUBLANES, LANES), F32)],
        grid=(N_TOK // TM,),
        in_specs=[ptile(D_MODEL), stile(D_MODEL), ptile(ATT_W), stile(ATT_W), ptile(UF_W), stile(UF_W),
                  lay3(woa), lay3(wof),
                  _mod_spec(layer, 2, _merged_row), _mod_spec(layer, 3, _merged_row),
                  _mod_spec(layer, 4, _merged_row), lay3(g2), lay3(wr_hi), lay3(wr_lo), lay3(br),
                  pl.BlockSpec(ltri.shape, lambda i: (0, 0))],
        out_specs=[tile(D_MODEL), tile(D_MODEL), tile(LANES),
                   pl.BlockSpec((SUBLANES, LANES), lambda i: (0, 0))],
        scratch_shapes=[pltpu.VMEM((SUBLANES, LANES), F32)],
        compiler_params=_cparams(("arbitrary",)),
        name="post_attention_router",
    )(*x_pair, *att_pair, *fo_pair, woa, wof, mod5, mod5, mod5, g2, wr_hi, wr_lo, br, ltri)


def _sort_plan(route, counts):
    cnt = counts[0, N_GROUPS:N_GROUPS + N_EXPERTS].astype(jnp.int32)
    padded = ((cnt + TMM - 1) // TMM) * TMM
    ends = jnp.cumsum(padded)
    starts = ends - padded
    ids = jnp.arange(N_EXPERTS, dtype=jnp.int32)
    expert = route[:, 2:4].astype(jnp.int32)
    rank = route[:, 4:6].astype(jnp.int32)
    start_of = jnp.sum(jnp.where(expert[..., None] == ids, starts, 0), axis=-1)
    pos = (start_of + rank).reshape(N_TOK // TM, TM, 2).transpose(0, 2, 1).reshape(N_TOK // TM, 1, 2 * TM)
    tile_start = jnp.arange(N_MTILES, dtype=jnp.int32) * TMM
    n_active = ends[-1] // TMM
    tile = jnp.minimum(jnp.arange(N_MTILES, dtype=jnp.int32), jnp.maximum(n_active - 1, 0))
    te = jnp.sum((ends[None, :] <= (tile * TMM)[:, None]).astype(jnp.int32), axis=1)
    te = jnp.minimum(te, N_EXPERTS - 1)
    active = (tile_start < ends[-1]).astype(jnp.int32)
    is_last = jnp.sum(((ends[None, :] - TMM) == tile_start[:, None]) & (padded[None, :] > 0), axis=1)
    zero_tile = ((is_last > 0) | (active == 0)).astype(jnp.int32)
    return pos, te, tile, active, zero_tile


def _dispatch_kernel(zero_ref, pos_ref, h2_ref, xs_hbm, zbuf_ref, sem):
    i = pl.program_id(0)

    @pl.when(i == 0)
    def _():
        zbuf_ref[...] = jnp.zeros_like(zbuf_ref)
        tile_copy = lambda j: pltpu.make_async_copy(zbuf_ref, xs_hbm.at[pl.ds(j * TMM, TMM)], sem.at[1])

        def zstart(j, c):
            @pl.when(zero_ref[j] == 1)
            def _():
                tile_copy(j).start()
            return c

        def zwait(j, c):
            @pl.when(zero_ref[j] == 1)
            def _():
                tile_copy(j).wait()
            return c

        lax.fori_loop(0, N_MTILES, zstart, 0)
        lax.fori_loop(0, N_MTILES, zwait, 0)

    def copy(g, k, dst_row):
        return pltpu.make_async_copy(h2_ref.at[g, pl.ds(k, 1)], xs_hbm.at[pl.ds(dst_row, 1)], sem.at[0])

    for s in range(2):
        _for_each_row(TM, lambda a, g, k: copy(g, k, pos_ref[0, 0, s * TM + a]).start(priority=k % 2))
    for s in range(2):
        _for_each_row(TM, lambda a, g, k: copy(0, k, 0).wait())


def _for_each_row(n_rows, fn):
    def body(g, c):
        for k in range(SUBLANES):
            fn(g * SUBLANES + k, g, k)
        return c
    lax.fori_loop(0, n_rows // SUBLANES, body, 0)


def _dispatch(zero_tile, pos_tiles, h2):
    grid_spec = pltpu.PrefetchScalarGridSpec(
        num_scalar_prefetch=1,
        grid=(N_TOK // TM,),
        in_specs=[pl.BlockSpec((1, 1, 2 * TM), lambda i, z: (i, 0, 0), memory_space=pltpu.SMEM),
                  pl.BlockSpec((TM // SUBLANES, SUBLANES, D_MODEL), lambda i, z: (i, 0, 0))],
        out_specs=pl.BlockSpec(memory_space=pl.ANY),
        scratch_shapes=[pltpu.VMEM((TMM, D_MODEL), F32), pltpu.SemaphoreType.DMA((2,))],
    )
    return pl.pallas_call(
        _dispatch_kernel,
        out_shape=jax.ShapeDtypeStruct((P_ROWS, D_MODEL), F32),
        grid_spec=grid_spec,
        compiler_params=_cparams(("arbitrary",)),
        name="moe_dispatch",
    )(zero_tile, pos_tiles, h2.reshape(N_TOK // SUBLANES, SUBLANES, D_MODEL))


def _moe_kernel(te_ref, tile_ref, act_ref, xs_ref, wg_ref, wu_ref, wd_ref, ys_ref,
                wgb_ref, wub_ref, wdb_ref):
    j = pl.program_id(0)
    prev = te_ref[jnp.maximum(j - 1, 0)]

    @pl.when((j == 0) | (te_ref[j] != prev))
    def _():
        wgb_ref[...] = wg_ref[...].astype(BF16)
        wub_ref[...] = wu_ref[...].astype(BF16)
        wdb_ref[...] = wd_ref[...].astype(BF16)

    @pl.when(act_ref[j] == 1)
    def _():
        x = xs_ref[...].astype(BF16)
        hg = _dot(x, wgb_ref[...])
        hu = _dot(x, wub_ref[...])
        a = (hg / (1.0 + jnp.exp(-hg))) * hu
        ys_ref[...] = _dot(a.astype(BF16), wdb_ref[...])

    @pl.when(act_ref[j] == 0)
    def _():
        ys_ref[...] = jnp.zeros_like(ys_ref)


def _moe(layer, te, tile, active, xs, w_gate, w_up, w_down):
    wspec = lambda shape: pl.BlockSpec((None, None) + shape, lambda j, te, tl, act: (layer, te[j], 0, 0))
    grid_spec = pltpu.PrefetchScalarGridSpec(
        num_scalar_prefetch=3,
        grid=(N_MTILES,),
        in_specs=[
            pl.BlockSpec((TMM, D_MODEL), lambda j, te, tl, act: (tl[j], 0)),
            wspec((D_MODEL, D_EXPERT)), wspec((D_MODEL, D_EXPERT)), wspec((D_EXPERT, D_MODEL)),
        ],
        out_specs=pl.BlockSpec((TMM, D_MODEL), lambda j, te, tl, act: (j, 0)),
        scratch_shapes=[pltpu.VMEM((D_MODEL, D_EXPERT), BF16), pltpu.VMEM((D_MODEL, D_EXPERT), BF16),
                        pltpu.VMEM((D_EXPERT, D_MODEL), BF16)],
    )
    return pl.pallas_call(
        _moe_kernel,
        out_shape=jax.ShapeDtypeStruct((P_ROWS, D_MODEL), F32),
        grid_spec=grid_spec,
        compiler_params=_cparams(("arbitrary",)),
        name="moe_grouped_matmul",
    )(te, tile, active, xs, w_gate, w_up, w_down)


def _combine_kernel(pos_ref, ys_hbm, xn_ref, route_ref, gt2_ref, *rest, final):
    o_ref, ybuf_ref, sem = rest[-3:]

    def copy(g, k, src_row):
        return pltpu.make_async_copy(ys_hbm.at[pl.ds(src_row, 1)], ybuf_ref.at[g, pl.ds(k, 1)], sem.at[0])

    _for_each_row(2 * TM, lambda a, g, k: copy(g, k, pos_ref[0, 0, a]).start(priority=k % 2))
    _for_each_row(2 * TM, lambda a, g, k: copy(g, k, 0).wait())

    r = route_ref[...]
    yb = ybuf_ref[...].reshape(2 * TM, D_MODEL)
    y = r[:, 0:1] * yb[:TM] + r[:, 1:2] * yb[TM:]
    x = xn_ref[...] + gt2_ref[...] * y
    if final:
        x = _rms(x, rest[0][...])
    o_ref[...] = x


def _combine(layer, xn, ys, pos, route, mod5, final_g, *, sample):
    final = final_g is not None
    if sample:
        n_rows, off = N_SAMPLE, N_PROMPT // TM
        row_fn = lambda i: 1 + (i * TM) // DEC_SEQ
    else:
        n_rows, off, row_fn = N_PROMPT, 0, (lambda i: 0)
    tile = lambda w: pl.BlockSpec((TM, w), lambda i: (i + off, 0))
    in_specs = [pl.BlockSpec((1, 1, 2 * TM), lambda i: (i + off, 0, 0), memory_space=pltpu.SMEM),
                pl.BlockSpec(memory_space=pl.ANY),
                tile(D_MODEL), tile(LANES), _mod_spec(layer, 5, row_fn)]
    args = [pos, ys, xn, route, mod5]
    if final:
        in_specs.append(pl.BlockSpec((1, D_MODEL), lambda i: (0, 0)))
        args.append(final_g)
    return pl.pallas_call(
        functools.partial(_combine_kernel, final=final),
        out_shape=jax.ShapeDtypeStruct((n_rows, D_MODEL), F32),
        grid=(n_rows // TM,),
        in_specs=in_specs,
        out_specs=pl.BlockSpec((TM, D_MODEL), lambda i: (i, 0)),
        scratch_shapes=[pltpu.VMEM((2 * TM // SUBLANES, SUBLANES, D_MODEL), F32),
                        pltpu.SemaphoreType.DMA((1,))],
        compiler_params=_cparams(("arbitrary",)),
        name="moe_combine_" + ("sample" if sample else "prompt") + ("_final" if final else ""),
    )(*args)


def _pad_segments(w, axis, nseg, seg, segp):
    shape = w.shape
    w = w.reshape(shape[:axis] + (nseg, seg) + shape[axis + 1:])
    pad = [(0, 0)] * w.ndim
    pad[axis + 1] = (0, segp - seg)
    w = jnp.pad(w, pad)
    return w.reshape(shape[:axis] + (nseg * segp,) + shape[axis + 1:])


def _axial_angles(seq, dim):
    rows = seq // GRID_W
    r = jnp.repeat(jnp.arange(rows), GRID_W).astype(F32)
    col = jnp.tile(jnp.arange(GRID_W), rows).astype(F32)
    n = dim // 4
    freqs = ROPE_THETA ** (-jnp.arange(n, dtype=F32) / n)
    return r[:, None] * freqs, col[:, None] * freqs


def _rope_tables(seq, dim, slot, nslots):
    ang_r, ang_c = _axial_angles(seq, dim)
    n = dim // 4
    z = jnp.zeros((seq, n), F32)
    pad = jnp.zeros((seq, slot - dim), F32)
    cr, sr, cc, sc = jnp.cos(ang_r), jnp.sin(ang_r), jnp.cos(ang_c), jnp.sin(ang_c)
    c = jnp.concatenate([cr, cr, cc, cc, pad], axis=1)
    sa = jnp.concatenate([-sr, z, -sc, z, pad], axis=1)
    sb = jnp.concatenate([z, sr, z, sc, pad], axis=1)
    return [jnp.tile(t, (1, nslots)) for t in (c, sa, sb)]


def _dft_tables(seq):
    t = jnp.arange(seq, dtype=jnp.int32)
    ang = ((t[:, None] * t[None, :]) % seq).astype(F32) * (2.0 * math.pi / seq)
    return jnp.concatenate([jnp.cos(ang), -jnp.sin(ang)], axis=1).astype(BF16)


def _channel_dft():
    c = jnp.arange(FOURIER_DIM, dtype=jnp.int32)
    ang = ((c[:, None] * c[None, :]) % FOURIER_DIM).astype(F32) * (2.0 * math.pi / FOURIER_DIM)
    eye = jnp.eye(FOURIER_GROUPS, dtype=F32)
    return jnp.concatenate([jnp.kron(eye, jnp.cos(ang)), jnp.kron(eye, jnp.sin(ang))],
                           axis=1).astype(BF16)


def kernel(x_prompt, x_sample, cache_attn_k, cache_attn_v, cache_diff_k, cache_diff_v, c, c_ctx,
           norm1_g, w_mod, b_mod, w_in, w_out, q_norm_g, k_norm_g,
           lambda_q1, lambda_k1, lambda_q2, lambda_k2, subln_g, norm2_g,
           w_grp, b_grp, w_exp, b_exp, w_gate, w_up, w_down, final_g):
    cv = jnp.concatenate([c_ctx[None, :], c, jnp.zeros((MOD_ROWS - 1 - DEC_BATCH, D_MODEL), F32)], axis=0)
    mod5 = _modulation(cv, w_mod, b_mod).reshape(DEPTH, MOD_ROWS, N_MOD, 1, D_MODEL)

    o0, o1, o2, o3, o4 = QA_W + 2 * KA_W, QA_W + 2 * KA_W + 384, QA_W + 2 * KA_W + 768, 1792, 2048
    w_in_r = jnp.concatenate([
        w_in[..., :o0],
        _pad_segments(w_in[..., o0:o1], 2, 2 * DIFF_HEADS, DIFF_QK_DIM, DQK_SLOT),
        _pad_segments(w_in[..., o1:o2], 2, 2 * DIFF_HEADS, DIFF_QK_DIM, DQK_SLOT),
        _pad_segments(w_in[..., o2:o3], 2, DIFF_HEADS, DIFF_V_DIM, LANES),
        w_in[..., o3:o4]], axis=-1).astype(BF16)
    woa = jnp.concatenate([w_out[:, :QA_W],
                           _pad_segments(w_out[:, QA_W:QA_W + DIFF_HEADS * DIFF_V_DIM], 1,
                                         DIFF_HEADS, DIFF_V_DIM, LANES)], axis=1).astype(BF16)
    wof = w_out[:, QA_W + DIFF_HEADS * DIFF_V_DIM:].astype(BF16)
    g1 = norm1_g.reshape(DEPTH, 1, D_MODEL)
    g2 = norm2_g.reshape(DEPTH, 1, D_MODEL)
    gq = jnp.tile(q_norm_g, (1, ATTN_Q_HEADS)).reshape(DEPTH, 1, QA_W)
    gk = jnp.tile(k_norm_g, (1, ATTN_KV_HEADS)).reshape(DEPTH, 1, KA_W)
    subg = _pad_segments(jnp.tile(subln_g, (1, DIFF_HEADS)), 1, DIFF_HEADS, DIFF_V_DIM, LANES
                         ).reshape(DEPTH, 1, VD_W)
    lam_in = jnp.stack([lambda_q1, lambda_k1, lambda_q2, lambda_k2], axis=1)
    w_r = jnp.concatenate([w_grp, w_exp, jnp.zeros((DEPTH, D_MODEL, LANES - N_ROUTE), F32)], axis=-1)
    wr_hi = w_r.astype(BF16)
    wr_lo = (w_r - wr_hi.astype(F32)).astype(BF16)
    br = jnp.concatenate([b_grp, b_exp, jnp.zeros((DEPTH, LANES - N_ROUTE), F32)], axis=-1
                         ).reshape(DEPTH, 1, LANES)
    seg = np.arange(QA_W) // HEAD_DIM
    bd64 = jnp.asarray((seg[:, None] == seg[None, :]).astype(np.float32) / HEAD_DIM, dtype=BF16)
    ltri = jnp.asarray(np.tril(np.ones((TM, TM), np.float32), -1), dtype=BF16)
    rope_tabs = (_rope_tables(DEC_SEQ, HEAD_DIM, HEAD_DIM, ATTN_Q_HEADS)
                 + _rope_tables(DEC_SEQ, DIFF_QK_DIM, DQK_SLOT, 2 * DIFF_HEADS))
    cs64 = _channel_dft()
    tw_p = _dft_tables(SEQ)
    tw_s = _dft_tables(DEC_SEQ)
    pka = cache_attn_k.reshape(DEC_BATCH, DEPTH, PAST_LEN, KA_W)
    pva = cache_attn_v.reshape(DEC_BATCH, DEPTH, PAST_LEN, KA_W)
    pkd = _pad_segments(cache_diff_k.reshape(DEC_BATCH, DEPTH, PAST_LEN, 2 * DIFF_HEADS * DIFF_QK_DIM),
                        3, 2 * DIFF_HEADS, DIFF_QK_DIM, DQK_SLOT)
    pvd = _pad_segments(cache_diff_v.reshape(DEC_BATCH, DEPTH, PAST_LEN, DIFF_HEADS * DIFF_V_DIM),
                        3, DIFF_HEADS, DIFF_V_DIM, LANES)
    prefix = [pka.astype(BF16), jnp.roll(pka, HEAD_DIM, axis=-1).astype(BF16),
              pva.astype(BF16), jnp.roll(pva, HEAD_DIM, axis=-1).astype(BF16),
              pkd.astype(BF16), pvd.astype(BF16)]

    caches = []
    x_p = x_prompt.reshape(N_PROMPT, D_MODEL)
    x_s = x_sample.reshape(N_SAMPLE, D_MODEL)
    for layer in range(DEPTH):
        pre_p = _pre_attention(layer, x_p, mod5, g1, w_in_r, gq, gk, bd64, None, sample=False)
        pre_s = _pre_attention(layer, x_s, mod5, g1, w_in_r, gq, gk, bd64, rope_tabs, sample=True)
        caches.append(pre_p[9:])
        att_p = _attention(layer, pre_p, None, lam_in, subg, sample=False)
        att_s = _attention(layer, pre_s, prefix, lam_in, subg, sample=True)
        fo_p = _fourier(pre_p[8], cs64, tw_p, sample=False)
        fo_s = _fourier(pre_s[8], cs64, tw_s, sample=True)
        xn, h2, route, counts = _post_attention(layer, (x_p, x_s), (att_p, att_s), (fo_p, fo_s),
                                                woa, wof, mod5, g2, wr_hi, wr_lo, br, ltri)
        pos, te, tile, active, zero_tile = _sort_plan(route, counts)
        xs = _dispatch(zero_tile, pos, h2)
        ys = _moe(layer, te, tile, active, xs, w_gate, w_up, w_down)
        fg = final_g.reshape(1, D_MODEL) if layer + 1 == DEPTH else None
        x_p = _combine(layer, xn, ys, pos, route, mod5, fg, sample=False)
        x_s = _combine(layer, xn, ys, pos, route, mod5, fg, sample=True)
    y_prompt, y_sample = x_p, x_s

    def stack(i, shape, keep):
        arrs = [cl[i].reshape(shape)[..., :keep] for cl in caches]
        return jnp.stack(arrs, axis=1)

    new_attn_k = stack(0, (BATCH, SEQ, ATTN_KV_HEADS, HEAD_DIM), HEAD_DIM)
    new_attn_v = stack(1, (BATCH, SEQ, ATTN_KV_HEADS, HEAD_DIM), HEAD_DIM)
    new_diff_k = stack(2, (BATCH, SEQ, DIFF_HEADS, 2, DQK_SLOT), DIFF_QK_DIM)
    new_diff_v = stack(3, (BATCH, SEQ, DIFF_HEADS, LANES), DIFF_V_DIM)
    return (y_prompt.reshape(BATCH, SEQ, D_MODEL), y_sample.reshape(DEC_BATCH, DEC_SEQ, D_MODEL),
            new_attn_k, new_attn_v, new_diff_k, new_diff_v)
```

```python
import functools
import math

import jax
import jax.numpy as jnp
import numpy as np
from jax import lax
from jax.experimental import pallas as pl
from jax.experimental.pallas import tpu as pltpu

F32 = jnp.float32
BF16 = jnp.bfloat16

D_MODEL = 1024
BATCH = 32
SEQ = 256
DEPTH = 2
DEC_BATCH = 4
DEC_SEQ = 2048
PAST_LEN = 256
GRID_W = 64
HEAD_DIM = 64
ATTN_Q_HEADS = 6
ATTN_KV_HEADS = 2
DIFF_HEADS = 4
DIFF_QK_DIM = 48
DIFF_V_DIM = 96
FOURIER_GROUPS = 4
FOURIER_DIM = 64
N_GROUPS = 4
EXPERTS_PER_GROUP = 8
N_EXPERTS = N_GROUPS * EXPERTS_PER_GROUP
D_EXPERT = 512
ROPE_THETA = 10000.0
EPS = 1e-6
LOG2E = math.log2(math.e)
N_MOD = 6

LANES = 128
SUBLANES = 8

N_PROMPT = BATCH * SEQ
N_SAMPLE = DEC_BATCH * DEC_SEQ
N_TOK = N_PROMPT + N_SAMPLE
QA_W = ATTN_Q_HEADS * HEAD_DIM
QS_W = ATTN_Q_HEADS * LANES
KA_W = ATTN_KV_HEADS * HEAD_DIM
DQK_SLOT = 64
QD_W = DIFF_HEADS * 2 * DQK_SLOT
VD_W = DIFF_HEADS * LANES
UF_W = FOURIER_GROUPS * FOURIER_DIM
P_W = QA_W + 2 * KA_W + 2 * QD_W + VD_W + UF_W
ATT_W = QA_W + VD_W
MOD_ROWS = 8
N_ROUTE = N_GROUPS + N_EXPERTS

TM = 512
TQ_SAMPLE = 256
TQF = 512
TMM = 256
N_ASSIGN = 2 * N_TOK
P_ROWS = N_ASSIGN + N_EXPERTS * TMM
N_MTILES = P_ROWS // TMM
VMEM_LIMIT = 48 * 1024 * 1024
NEG = -1e30


def _cparams(sem):
    return pltpu.CompilerParams(dimension_semantics=sem, vmem_limit_bytes=VMEM_LIMIT)


def _dot(a, b):
    return jnp.dot(a, b, preferred_element_type=F32)


def _dot_nt(a, b):
    return lax.dot_general(a, b, (((1,), (1,)), ((), ())), preferred_element_type=F32)


def _split(x):
    hi = x.astype(BF16)
    lo = (x - hi.astype(F32)).astype(BF16)
    return hi, lo


def _rms(x, g):
    ms = jnp.mean(x * x, axis=-1, keepdims=True)
    return x * lax.rsqrt(ms + EPS) * g


def _mod_kernel(cv_ref, w_ref, b_ref, o_ref):
    cv = cv_ref[...]
    s = cv / (1.0 + jnp.exp(-cv))
    s_hi, s_lo = _split(s)
    w_hi, w_lo = _split(w_ref[...])
    o_ref[...] = _dot(s_hi, w_hi) + _dot(s_hi, w_lo) + _dot(s_lo, w_hi) + b_ref[...]


def _modulation(cv, w_mod, b_mod):
    tn = 1536
    return pl.pallas_call(
        _mod_kernel,
        out_shape=jax.ShapeDtypeStruct((DEPTH, MOD_ROWS, N_MOD * D_MODEL), F32),
        grid=(DEPTH, N_MOD * D_MODEL // tn),
        in_specs=[
            pl.BlockSpec((MOD_ROWS, D_MODEL), lambda l, n: (0, 0)),
            pl.BlockSpec((None, D_MODEL, tn), lambda l, n: (l, 0, n)),
            pl.BlockSpec((None, 1, tn), lambda l, n: (l, 0, n)),
        ],
        out_specs=pl.BlockSpec((None, MOD_ROWS, tn), lambda l, n: (l, 0, n)),
        compiler_params=_cparams(("arbitrary", "arbitrary")),
        name="modulation",
    )(cv, w_mod, b_mod.reshape(DEPTH, 1, N_MOD * D_MODEL))


def _mod_spec(layer, chunk, row_fn):
    return pl.BlockSpec((None, None, None, 1, D_MODEL),
                        lambda i: (layer, row_fn(i), chunk, 0, 0))


def _merged_row(i):
    n_p = N_PROMPT // TM
    return jnp.where(i < n_p, 0, 1 + ((i - n_p) * TM) // DEC_SEQ)


def _segmean(x2, bd):
    hi, lo = _split(x2)
    return _dot(hi, bd) + _dot(lo, bd)


def _rope(x, c, s, n, slot):
    w = x.shape[1]
    pos = lax.broadcasted_iota(jnp.int32, x.shape, 1) % slot
    first = (pos < n) | ((pos >= 2 * n) & (pos < 3 * n))
    partner = jnp.where(first, -pltpu.roll(x, w - n, 1), pltpu.roll(x, n, 1))
    return x * c + partner * s


def _pre_kernel(*refs, rope, caches):
    x_ref, sh_ref, sc_ref, g1_ref, w_ref, gq_ref, gk_ref, bd_ref = refs[:8]
    pos = 8
    if rope:
        ca_ref, sa_ref, cd_ref, sd_ref = refs[pos:pos + 4]
        pos += 4
    qs_ref, ka_ref, va_ref, qd1_ref, qd2_ref, kd_ref, vd_ref, uf_ref = refs[pos:pos + 8]
    pos += 8
    if caches:
        cka_ref, cva_ref, ckd_ref, cvd_ref = refs[pos:pos + 4]

    x = x_ref[...]
    h = _rms(x, g1_ref[...]) * (1.0 + sc_ref[...]) + sh_ref[...]
    p = _dot(h.astype(BF16), w_ref[...])
    o = 0
    qa = p[:, o:o + QA_W]; o += QA_W
    ka = p[:, o:o + KA_W]; o += KA_W
    va = p[:, o:o + KA_W]; o += KA_W
    qd = p[:, o:o + QD_W]; o += QD_W
    kd = p[:, o:o + QD_W]; o += QD_W
    vd = p[:, o:o + VD_W]; o += VD_W
    uf = p[:, o:o + UF_W]

    bd = bd_ref[...]
    qa = qa * lax.rsqrt(_segmean(qa * qa, bd) + EPS) * gq_ref[...]
    ka = ka * lax.rsqrt(_segmean(ka * ka, bd[:KA_W, :KA_W]) + EPS) * gk_ref[...]
    if caches:
        cka_ref[...] = ka
        cva_ref[...] = va
        ckd_ref[...] = kd
        cvd_ref[...] = vd
    if rope:
        ca, sa, cd, sd = ca_ref[...], sa_ref[...], cd_ref[...], sd_ref[...]
        qa = _rope(qa, ca, sa, HEAD_DIM // 4, HEAD_DIM)
        ka = _rope(ka, ca[:, :KA_W], sa[:, :KA_W], HEAD_DIM // 4, HEAD_DIM)
        qd = _rope(qd, cd, sd, DIFF_QK_DIM // 4, DQK_SLOT)
        kd = _rope(kd, cd, sd, DIFF_QK_DIM // 4, DQK_SLOT)
    qa = qa * (HEAD_DIM ** -0.5 * LOG2E)
    qd = qd * (DIFF_QK_DIM ** -0.5 * LOG2E)
    nat_lo, nat_hi = _lane_halves(qa)
    left_lo, _ = _lane_halves(pltpu.roll(qa, QA_W - HEAD_DIM, 1))
    _, right_hi = _lane_halves(pltpu.roll(qa, HEAD_DIM, 1))
    cols = (nat_lo[:, 0:LANES], left_lo[:, 0:LANES], nat_lo[:, LANES:2 * LANES],
            nat_hi[:, LANES:2 * LANES], right_hi[:, 2 * LANES:], nat_hi[:, 2 * LANES:])
    for h, col in enumerate(cols):
        qs_ref[:, h * LANES:(h + 1) * LANES] = col.astype(BF16)
    qd_lo, qd_hi = _lane_halves(qd)
    qd1_ref[...] = qd_lo.astype(BF16)
    qd2_ref[...] = qd_hi.astype(BF16)
    ka_ref[...] = ka.astype(BF16)
    va_ref[...] = va.astype(BF16)
    kd_ref[...] = kd.astype(BF16)
    vd_ref[...] = vd.astype(BF16)
    uf_ref[...] = uf.astype(BF16)


def _lane_halves(x):
    lo = (lax.broadcasted_iota(jnp.int32, x.shape, 1) % LANES) < HEAD_DIM
    z = jnp.zeros_like(x)
    return jnp.where(lo, x, z), jnp.where(lo, z, x)


def _pre_attention(layer, x_half, mod5, g1, w_in_r, gq, gk, bd64, rope_tabs, *, sample):
    n_rows = N_SAMPLE if sample else N_PROMPT
    if sample:
        grid = (DEC_SEQ // TM, DEC_BATCH)
        blk = lambda p, b: b * (DEC_SEQ // TM) + p
        row_fn = lambda p, b: 1 + b
    else:
        grid = (n_rows // TM, 1)
        blk = lambda p, b: p
        row_fn = lambda p, b: 0
    mod_spec = lambda chunk: pl.BlockSpec((None, None, None, 1, D_MODEL),
                                          lambda p, b: (layer, row_fn(p, b), chunk, 0, 0))
    tile = lambda w: pl.BlockSpec((TM, w), lambda p, b: (blk(p, b), 0))
    lay3 = lambda a: pl.BlockSpec((None,) + a.shape[1:], lambda p, b: (layer, 0, 0))
    in_specs = [tile(D_MODEL), mod_spec(0), mod_spec(1), lay3(g1), lay3(w_in_r), lay3(gq), lay3(gk),
                pl.BlockSpec(bd64.shape, lambda p, b: (0, 0))]
    args = [x_half, mod5, mod5, g1, w_in_r, gq, gk, bd64]
    if sample:
        for t in rope_tabs:
            in_specs.append(pl.BlockSpec((TM, t.shape[1]), lambda p, b: (p, 0)))
            args.append(t)
    widths = (QS_W, KA_W, KA_W, QD_W, QD_W, QD_W, VD_W, UF_W)
    out_shape = [jax.ShapeDtypeStruct((n_rows, w), BF16) for w in widths]
    out_specs = [tile(w) for w in widths]
    if not sample:
        for w in (KA_W, KA_W, QD_W, VD_W):
            out_shape.append(jax.ShapeDtypeStruct((n_rows, w), F32))
            out_specs.append(tile(w))
    return pl.pallas_call(
        functools.partial(_pre_kernel, rope=sample, caches=not sample),
        out_shape=out_shape,
        grid=grid,
        in_specs=in_specs,
        out_specs=out_specs,
        compiler_params=_cparams(("arbitrary", "arbitrary")),
        name="pre_attention_sample" if sample else "pre_attention_prompt",
    )(*args)


def _softmax_pv(q_rows, k_pieces, v_pieces):
    s = [_dot_nt(q_rows, k) for k in k_pieces]
    m = s[0].max(axis=-1, keepdims=True)
    for si in s[1:]:
        m = jnp.maximum(m, si.max(axis=-1, keepdims=True))
    acc = None
    den = None
    for si, v in zip(s, v_pieces):
        e = jnp.exp2(si - m)
        d = e.sum(axis=-1, keepdims=True)
        o = _dot(e.astype(BF16), v)
        acc = o if acc is None else acc + o
        den = d if den is None else den + d
    return acc * (1.0 / den)


def _attn_kernel(*refs, n_pieces, lam_init):
    qs_ref, qd1_ref, qd2_ref = refs[:3]
    pos = 3
    pieces = []
    for _ in range(n_pieces):
        pieces.append(refs[pos:pos + 4])
        pos += 4
    lam_ref, subg_ref, o_ref = refs[pos:pos + 3]
    tq = o_ref.shape[0]

    lamv = lam_ref[...]
    s1 = jnp.sum(lamv[0:1] * lamv[1:2], axis=-1, keepdims=True)
    s2 = jnp.sum(lamv[2:3] * lamv[3:4], axis=-1, keepdims=True)
    lam = jnp.exp(s1) - jnp.exp(s2) + lam_init

    blocks = []
    for kv in range(ATTN_KV_HEADS):
        group = ATTN_Q_HEADS // ATTN_KV_HEADS
        q_rows = jnp.concatenate([qs_ref[:, (kv * group + g) * LANES:(kv * group + g + 1) * LANES]
                                  for g in range(group)], axis=0)
        o = _softmax_pv(q_rows, [pc[0][...] for pc in pieces], [pc[1][...] for pc in pieces])
        blocks.append([o[g * tq:(g + 1) * tq] for g in range(group)])
    lo = lax.broadcasted_iota(jnp.int32, (tq, LANES), 1) < HEAD_DIM
    swap = lambda x: pltpu.roll(x, HEAD_DIM, 1)
    b, c = blocks
    cols = (jnp.where(lo, b[0], swap(b[1])), jnp.where(lo, b[2], c[0]), jnp.where(lo, swap(c[1]), c[2]))
    for j, col in enumerate(cols):
        o_ref[:, j * LANES:(j + 1) * LANES] = col.astype(BF16)

    for h in range(DIFF_HEADS):
        sl = slice(h * LANES, (h + 1) * LANES)
        q_rows = jnp.concatenate([qd1_ref[:, sl], qd2_ref[:, sl]], axis=0)
        o = _softmax_pv(q_rows, [pc[2][:, sl] for pc in pieces], [pc[3][:, sl] for pc in pieces])
        od = o[:tq] - lam * o[tq:]
        ms = jnp.sum(od * od, axis=-1, keepdims=True) * (1.0 / DIFF_V_DIM)
        od = od * lax.rsqrt(ms + EPS) * subg_ref[:, sl] * (1.0 - lam_init)
        o_ref[:, QA_W + h * LANES:QA_W + (h + 1) * LANES] = od.astype(BF16)


def _attention(layer, pre, prefix, lam_in, subg, *, sample):
    qs, ka, va, qd1, qd2, kd, vd = pre[:7]
    lam_init = 0.8 - 0.6 * math.exp(-0.3 * layer)
    if sample:
        nb, seq, tq = DEC_BATCH, DEC_SEQ, TQ_SAMPLE
    else:
        nb, seq, tq = BATCH, SEQ, SEQ
    nq = seq // tq
    qspec = lambda w: pl.BlockSpec((tq, w), lambda b, q: (b * nq + q, 0))
    in_specs = [qspec(QS_W), qspec(QD_W), qspec(QD_W)]
    args = [qs, qd1, qd2]
    n_pieces = 1
    if sample:
        n_pieces = 2
        for a in prefix:
            in_specs.append(pl.BlockSpec((None, None) + a.shape[2:], lambda b, q: (b, layer, 0, 0)))
            args.append(a)
    for a, w in ((ka, KA_W), (va, KA_W), (kd, QD_W), (vd, VD_W)):
        in_specs.append(pl.BlockSpec((seq, w), lambda b, q: (b, 0)))
        args.append(a)
    in_specs += [pl.BlockSpec((None, 4, DIFF_QK_DIM), lambda b, q: (layer, 0, 0)),
                 pl.BlockSpec((None, 1, VD_W), lambda b, q: (layer, 0, 0))]
    args += [lam_in, subg]
    return pl.pallas_call(
        functools.partial(_attn_kernel, n_pieces=n_pieces, lam_init=lam_init),
        out_shape=jax.ShapeDtypeStruct((nb * seq, ATT_W), BF16),
        grid=(nb, nq),
        in_specs=in_specs,
        out_specs=pl.BlockSpec((tq, ATT_W), lambda b, q: (b * nq + q, 0)),
        compiler_params=_cparams(("arbitrary", "arbitrary")),
        name="attention_sample" if sample else "attention_prompt",
    )(*args)


def _fourier_kernel(u_ref, cs_ref, tw_ref, o_ref, ab_ref, *, seq, norm):
    @pl.when(pl.program_id(1) == 0)
    def _():
        ab = _dot(u_ref[...], cs_ref[...])
        ab_ref[0:seq, :] = ab[:, :UF_W].astype(BF16)
        ab_ref[seq:2 * seq, :] = ab[:, UF_W:].astype(BF16)

    o_ref[...] = (_dot(tw_ref[...], ab_ref[...]) * norm).astype(BF16)


def _fourier(uf, cs64, tw, *, sample):
    if sample:
        nb, seq, tq = DEC_BATCH, DEC_SEQ, TQF
    else:
        nb, seq, tq = BATCH, SEQ, SEQ
    nq = seq // tq
    in_specs = [pl.BlockSpec((seq, UF_W), lambda b, q: (b, 0)),
                pl.BlockSpec(cs64.shape, lambda b, q: (0, 0)),
                pl.BlockSpec((tq, 2 * seq), lambda b, q: (q, 0))]
    return pl.pallas_call(
        functools.partial(_fourier_kernel, seq=seq, norm=(seq * FOURIER_DIM) ** -0.5),
        out_shape=jax.ShapeDtypeStruct((nb * seq, UF_W), BF16),
        grid=(nb, nq),
        in_specs=in_specs,
        out_specs=pl.BlockSpec((tq, UF_W), lambda b, q: (b * nq + q, 0)),
        scratch_shapes=[pltpu.VMEM((2 * seq, UF_W), BF16)],
        compiler_params=_cparams(("arbitrary", "arbitrary")),
        name="fourier_sample" if sample else "fourier_prompt",
    )(uf, cs64, tw)


def _post_kernel(xp_ref, xs_ref, attp_ref, atts_ref, fop_ref, fos_ref, woa_ref, wof_ref,
                 gt1_ref, sh2_ref, sc2_ref, g2_ref, wrh_ref, wrl_ref, br_ref, ltri_ref,
                 xn_ref, h2_ref, route_ref, cnt_ref, carry_ref):
    @pl.when(pl.program_id(0) == 0)
    def _():
        carry_ref[...] = jnp.zeros_like(carry_ref)

    is_prompt = pl.program_id(0) < N_PROMPT // TM
    pick = lambda p_ref, s_ref: jnp.where(is_prompt, p_ref[...], s_ref[...])
    mix = (_dot(pick(attp_ref, atts_ref), woa_ref[...])
           + _dot(pick(fop_ref, fos_ref), wof_ref[...]))
    xn = pick(xp_ref, xs_ref) + gt1_ref[...] * mix
    xn_ref[...] = xn
    h2 = _rms(xn, g2_ref[...]) * (1.0 + sc2_ref[...]) + sh2_ref[...]
    h2_ref[...] = h2

    h_hi, h_lo = _split(h2)
    wrh = wrh_ref[...]
    logits = _dot(h_hi, wrh) + _dot(h_hi, wrl_ref[...]) + _dot(h_lo, wrh) + br_ref[...]
    lane = lax.broadcasted_iota(jnp.int32, logits.shape, 1).astype(F32)
    big = float(LANES)

    def first_max(v):
        m = v.max(axis=-1, keepdims=True)
        idx = jnp.where(v == m, lane, big).min(axis=-1, keepdims=True)
        return m, idx

    lg = jnp.where(lane < N_GROUPS, logits, NEG)
    mg, gi = first_max(lg)
    pg_top = 1.0 / jnp.exp(lg - mg).sum(axis=-1, keepdims=True)
    e_lo = N_GROUPS + gi * EXPERTS_PER_GROUP
    le = jnp.where((lane >= e_lo) & (lane < e_lo + EXPERTS_PER_GROUP), logits, NEG)
    m0, i0 = first_max(le)
    m1, i1 = first_max(jnp.where(lane == i0, NEG, le))
    e1 = jnp.exp(m1 - m0)
    inv = 1.0 / (1.0 + e1)
    w0 = pg_top * inv
    w1 = pg_top * (e1 * inv)

    hot0 = lane == i0
    hot1 = lane == i1
    onehot = jnp.where(hot0 | hot1, 1.0, 0.0)
    before = _dot(ltri_ref[...], onehot.astype(BF16)) + carry_ref[0:1, :]
    r0 = jnp.where(hot0, before, 0.0).sum(axis=-1, keepdims=True)
    r1 = jnp.where(hot1, before, 0.0).sum(axis=-1, keepdims=True)
    carry = carry_ref[...] + onehot.sum(axis=0, keepdims=True)
    carry_ref[...] = carry
    cnt_ref[...] = carry

    route = jnp.where(lane == 0, w0, 0.0)
    route = jnp.where(lane == 1, w1, route)
    route = jnp.where(lane == 2, i0 - N_GROUPS, route)
    route = jnp.where(lane == 3, i1 - N_GROUPS, route)
    route = jnp.where(lane == 4, r0, route)
    route = jnp.where(lane == 5, r1, route)
    route_ref[...] = route


def _post_attention(layer, x_pair, att_pair, fo_pair, woa, wof, mod5, g2, wr_hi, wr_lo, br, ltri):
    n_p = N_PROMPT // TM
    tile = lambda w: pl.BlockSpec((TM, w), lambda i: (i, 0))
    ptile = lambda w: pl.BlockSpec((TM, w), lambda i: (jnp.minimum(i, n_p - 1), 0))
    stile = lambda w: pl.BlockSpec((TM, w), lambda i: (jnp.maximum(i - n_p, 0), 0))
    lay3 = lambda a: pl.BlockSpec((None,) + a.shape[1:], lambda i: (layer, 0, 0))
    return pl.pallas_call(
        _post_kernel,
        out_shape=[jax.ShapeDtypeStruct((N_TOK, D_MODEL), F32),
                   jax.ShapeDtypeStruct((N_TOK, D_MODEL), F32),
                   jax.ShapeDtypeStruct((N_TOK, LANES), F32),
                   jax.ShapeDtypeStruct((SUBLANES, LANES), F32)],
        grid=(N_TOK // TM,),
        in_specs=[ptile(D_MODEL), stile(D_MODEL), ptile(ATT_W), stile(ATT_W), ptile(UF_W), stile(UF_W),
                  lay3(woa), lay3(wof),
                  _mod_spec(layer, 2, _merged_row), _mod_spec(layer, 3, _merged_row),
                  _mod_spec(layer, 4, _merged_row), lay3(g2), lay3(wr_hi), lay3(wr_lo), lay3(br),
                  pl.BlockSpec(ltri.shape, lambda i: (0, 0))],
        out_specs=[tile(D_MODEL), tile(D_MODEL), tile(LANES),
                   pl.BlockSpec((SUBLANES, LANES), lambda i: (0, 0))],
        scratch_shapes=[pltpu.VMEM((SUBLANES, LANES), F32)],
        compiler_params=_cparams(("arbitrary",)),
        name="post_attention_router",
    )(*x_pair, *att_pair, *fo_pair, woa, wof, mod5, mod5, mod5, g2, wr_hi, wr_lo, br, ltri)


def _sort_plan(route, counts):
    cnt = counts[0, N_GROUPS:N_GROUPS + N_EXPERTS].astype(jnp.int32)
    padded = ((cnt + TMM - 1) // TMM) * TMM
    ends = jnp.cumsum(padded)
    starts = ends - padded
    ids = jnp.arange(N_EXPERTS, dtype=jnp.int32)
    expert = route[:, 2:4].astype(jnp.int32)
    rank = route[:, 4:6].astype(jnp.int32)
    start_of = jnp.sum(jnp.where(expert[..., None] == ids, starts, 0), axis=-1)
    pos = (start_of + rank).reshape(N_TOK // TM, TM, 2).transpose(0, 2, 1).reshape(N_TOK // TM, 1, 2 * TM)
    tile_start = jnp.arange(N_MTILES, dtype=jnp.int32) * TMM
    n_active = ends[-1] // TMM
    tile = jnp.minimum(jnp.arange(N_MTILES, dtype=jnp.int32), jnp.maximum(n_active - 1, 0))
    te = jnp.sum((ends[None, :] <= (tile * TMM)[:, None]).astype(jnp.int32), axis=1)
    te = jnp.minimum(te, N_EXPERTS - 1)
    active = (tile_start < ends[-1]).astype(jnp.int32)
    is_last = jnp.sum(((ends[None, :] - TMM) == tile_start[:, None]) & (padded[None, :] > 0), axis=1)
    zero_tile = ((is_last > 0) | (active == 0)).astype(jnp.int32)
    return pos, te, tile, active, zero_tile


def _dispatch_kernel(zero_ref, pos_ref, h2_ref, xs_hbm, zbuf_ref, sem):
    i = pl.program_id(0)

    @pl.when(i == 0)
    def _():
        zbuf_ref[...] = jnp.zeros_like(zbuf_ref)
        tile_copy = lambda j: pltpu.make_async_copy(zbuf_ref, xs_hbm.at[pl.ds(j * TMM, TMM)], sem.at[1])

        def zstart(j, c):
            @pl.when(zero_ref[j] == 1)
            def _():
                tile_copy(j).start()
            return c

        def zwait(j, c):
            @pl.when(zero_ref[j] == 1)
            def _():
                tile_copy(j).wait()
            return c

        lax.fori_loop(0, N_MTILES, zstart, 0)
        lax.fori_loop(0, N_MTILES, zwait, 0)

    def copy(g, k, dst_row):
        return pltpu.make_async_copy(h2_ref.at[g, pl.ds(k, 1)], xs_hbm.at[pl.ds(dst_row, 1)], sem.at[0])

    for s in range(2):
        _for_each_row(TM, lambda a, g, k: copy(g, k, pos_ref[0, 0, s * TM + a]).start(priority=k % 2))
    for s in range(2):
        _for_each_row(TM, lambda a, g, k: copy(0, k, 0).wait())


def _for_each_row(n_rows, fn):
    def body(g, c):
        for k in range(SUBLANES):
            fn(g * SUBLANES + k, g, k)
        return c
    lax.fori_loop(0, n_rows // SUBLANES, body, 0)


def _dispatch(zero_tile, pos_tiles, h2):
    grid_spec = pltpu.PrefetchScalarGridSpec(
        num_scalar_prefetch=1,
        grid=(N_TOK // TM,),
        in_specs=[pl.BlockSpec((1, 1, 2 * TM), lambda i, z: (i, 0, 0), memory_space=pltpu.SMEM),
                  pl.BlockSpec((TM // SUBLANES, SUBLANES, D_MODEL), lambda i, z: (i, 0, 0))],
        out_specs=pl.BlockSpec(memory_space=pl.ANY),
        scratch_shapes=[pltpu.VMEM((TMM, D_MODEL), F32), pltpu.SemaphoreType.DMA((2,))],
    )
    return pl.pallas_call(
        _dispatch_kernel,
        out_shape=jax.ShapeDtypeStruct((P_ROWS, D_MODEL), F32),
        grid_spec=grid_spec,
        compiler_params=_cparams(("arbitrary",)),
        name="moe_dispatch",
    )(zero_tile, pos_tiles, h2.reshape(N_TOK // SUBLANES, SUBLANES, D_MODEL))


def _moe_kernel(te_ref, tile_ref, act_ref, xs_ref, wg_ref, wu_ref, wd_ref, ys_ref,
                wgb_ref, wub_ref, wdb_ref):
    j = pl.program_id(0)
    prev = te_ref[jnp.maximum(j - 1, 0)]

    @pl.when((j == 0) | (te_ref[j] != prev))
    def _():
        wgb_ref[...] = wg_ref[...].astype(BF16)
        wub_ref[...] = wu_ref[...].astype(BF16)
        wdb_ref[...] = wd_ref[...].astype(BF16)

    @pl.when(act_ref[j] == 1)
    def _():
        x = xs_ref[...].astype(BF16)
        hg = _dot(x, wgb_ref[...])
        hu = _dot(x, wub_ref[...])
        a = (hg / (1.0 + jnp.exp(-hg))) * hu
        ys_ref[...] = _dot(a.astype(BF16), wdb_ref[...])

    @pl.when(act_ref[j] == 0)
    def _():
        ys_ref[...] = jnp.zeros_like(ys_ref)


def _moe(layer, te, tile, active, xs, w_gate, w_up, w_down):
    wspec = lambda shape: pl.BlockSpec((None, None) + shape, lambda j, te, tl, act: (layer, te[j], 0, 0))
    grid_spec = pltpu.PrefetchScalarGridSpec(
        num_scalar_prefetch=3,
        grid=(N_MTILES,),
        in_specs=[
            pl.BlockSpec((TMM, D_MODEL), lambda j, te, tl, act: (tl[j], 0)),
            wspec((D_MODEL, D_EXPERT)), wspec((D_MODEL, D_EXPERT)), wspec((D_EXPERT, D_MODEL)),
        ],
        out_specs=pl.BlockSpec((TMM, D_MODEL), lambda j, te, tl, act: (j, 0)),
        scratch_shapes=[pltpu.VMEM((D_MODEL, D_EXPERT), BF16), pltpu.VMEM((D_MODEL, D_EXPERT), BF16),
                        pltpu.VMEM((D_EXPERT, D_MODEL), BF16)],
    )
    return pl.pallas_call(
        _moe_kernel,
        out_shape=jax.ShapeDtypeStruct((P_ROWS, D_MODEL), F32),
        grid_spec=grid_spec,
        compiler_params=_cparams(("arbitrary",)),
        name="moe_grouped_matmul",
    )(te, tile, active, xs, w_gate, w_up, w_down)


def _combine_kernel(pos_ref, ys_hbm, xn_ref, route_ref, gt2_ref, *rest, final):
    o_ref, ybuf_ref, sem = rest[-3:]

    def copy(g, k, src_row):
        return pltpu.make_async_copy(ys_hbm.at[pl.ds(src_row, 1)], ybuf_ref.at[g, pl.ds(k, 1)], sem.at[0])

    _for_each_row(2 * TM, lambda a, g, k: copy(g, k, pos_ref[0, 0, a]).start(priority=k % 2))
    _for_each_row(2 * TM, lambda a, g, k: copy(g, k, 0).wait())

    r = route_ref[...]
    yb = ybuf_ref[...].reshape(2 * TM, D_MODEL)
    y = r[:, 0:1] * yb[:TM] + r[:, 1:2] * yb[TM:]
    x = xn_ref[...] + gt2_ref[...] * y
    if final:
        x = _rms(x, rest[0][...])
    o_ref[...] = x


def _combine(layer, xn, ys, pos, route, mod5, final_g, *, sample):
    final = final_g is not None
    if sample:
        n_rows, off = N_SAMPLE, N_PROMPT // TM
        row_fn = lambda i: 1 + (i * TM) // DEC_SEQ
    else:
        n_rows, off, row_fn = N_PROMPT, 0, (lambda i: 0)
    tile = lambda w: pl.BlockSpec((TM, w), lambda i: (i + off, 0))
    in_specs = [pl.BlockSpec((1, 1, 2 * TM), lambda i: (i + off, 0, 0), memory_space=pltpu.SMEM),
                pl.BlockSpec(memory_space=pl.ANY),
                tile(D_MODEL), tile(LANES), _mod_spec(layer, 5, row_fn)]
    args = [pos, ys, xn, route, mod5]
    if final:
        in_specs.append(pl.BlockSpec((1, D_MODEL), lambda i: (0, 0)))
        args.append(final_g)
    return pl.pallas_call(
        functools.partial(_combine_kernel, final=final),
        out_shape=jax.ShapeDtypeStruct((n_rows, D_MODEL), F32),
        grid=(n_rows // TM,),
        in_specs=in_specs,
        out_specs=pl.BlockSpec((TM, D_MODEL), lambda i: (i, 0)),
        scratch_shapes=[pltpu.VMEM((2 * TM // SUBLANES, SUBLANES, D_MODEL), F32),
                        pltpu.SemaphoreType.DMA((1,))],
        compiler_params=_cparams(("arbitrary",)),
        name="moe_combine_" + ("sample" if sample else "prompt") + ("_final" if final else ""),
    )(*args)


def _pad_segments(w, axis, nseg, seg, segp):
    shape = w.shape
    w = w.reshape(shape[:axis] + (nseg, seg) + shape[axis + 1:])
    pad = [(0, 0)] * w.ndim
    pad[axis + 1] = (0, segp - seg)
    w = jnp.pad(w, pad)
    return w.reshape(shape[:axis] + (nseg * segp,) + shape[axis + 1:])


def _axial_angles(seq, dim):
    rows = seq // GRID_W
    r = jnp.repeat(jnp.arange(rows), GRID_W).astype(F32)
    col = jnp.tile(jnp.arange(GRID_W), rows).astype(F32)
    n = dim // 4
    freqs = ROPE_THETA ** (-jnp.arange(n, dtype=F32) / n)
    return r[:, None] * freqs, col[:, None] * freqs


def _rope_tables(seq, dim, slot, nslots):
    ang_r, ang_c = _axial_angles(seq, dim)
    n = dim // 4
    z = jnp.zeros((seq, n), F32)
    pad = jnp.zeros((seq, slot - dim), F32)
    cr, sr, cc, sc = jnp.cos(ang_r), jnp.sin(ang_r), jnp.cos(ang_c), jnp.sin(ang_c)
    c = jnp.concatenate([cr, cr, cc, cc, pad], axis=1)
    s = jnp.concatenate([sr, sr, sc, sc, pad], axis=1)
    return [jnp.tile(t, (1, nslots)) for t in (c, s)]


def _dft_tables(seq):
    lo_n = 64
    t = jnp.arange(seq, dtype=jnp.int32)
    unit = 2.0 * math.pi / seq

    def cs(mult):
        ang = ((mult[:, None] * t[None, :]) % seq).astype(F32) * unit
        return jnp.cos(ang), jnp.sin(ang)

    ch, sh = cs(jnp.arange(seq // lo_n, dtype=jnp.int32) * lo_n)
    cl, sl = cs(jnp.arange(lo_n, dtype=jnp.int32))
    ch, sh, cl, sl = ch[:, None, :], sh[:, None, :], cl[None, :, :], sl[None, :, :]
    cos = (ch * cl - sh * sl).reshape(seq, seq)
    sin = (sh * cl + ch * sl).reshape(seq, seq)
    return jnp.concatenate([cos, -sin], axis=1).astype(BF16)


def _channel_dft():
    c = jnp.arange(FOURIER_DIM, dtype=jnp.int32)
    ang = ((c[:, None] * c[None, :]) % FOURIER_DIM).astype(F32) * (2.0 * math.pi / FOURIER_DIM)
    eye = jnp.eye(FOURIER_GROUPS, dtype=F32)
    return jnp.concatenate([jnp.kron(eye, jnp.cos(ang)), jnp.kron(eye, jnp.sin(ang))],
                           axis=1).astype(BF16)


def kernel(x_prompt, x_sample, cache_attn_k, cache_attn_v, cache_diff_k, cache_diff_v, c, c_ctx,
           norm1_g, w_mod, b_mod, w_in, w_out, q_norm_g, k_norm_g,
           lambda_q1, lambda_k1, lambda_q2, lambda_k2, subln_g, norm2_g,
           w_grp, b_grp, w_exp, b_exp, w_gate, w_up, w_down, final_g):
    cv = jnp.concatenate([c_ctx[None, :], c, jnp.zeros((MOD_ROWS - 1 - DEC_BATCH, D_MODEL), F32)], axis=0)
    mod5 = _modulation(cv, w_mod, b_mod).reshape(DEPTH, MOD_ROWS, N_MOD, 1, D_MODEL)

    o0, o1, o2, o3, o4 = QA_W + 2 * KA_W, QA_W + 2 * KA_W + 384, QA_W + 2 * KA_W + 768, 1792, 2048
    w_in_r = jnp.concatenate([
        w_in[..., :o0],
        _pad_segments(w_in[..., o0:o1], 2, 2 * DIFF_HEADS, DIFF_QK_DIM, DQK_SLOT),
        _pad_segments(w_in[..., o1:o2], 2, 2 * DIFF_HEADS, DIFF_QK_DIM, DQK_SLOT),
        _pad_segments(w_in[..., o2:o3], 2, DIFF_HEADS, DIFF_V_DIM, LANES),
        w_in[..., o3:o4]], axis=-1).astype(BF16)
    woa = jnp.concatenate([w_out[:, :QA_W],
                           _pad_segments(w_out[:, QA_W:QA_W + DIFF_HEADS * DIFF_V_DIM], 1,
                                         DIFF_HEADS, DIFF_V_DIM, LANES)], axis=1).astype(BF16)
    wof = w_out[:, QA_W + DIFF_HEADS * DIFF_V_DIM:].astype(BF16)
    g1 = norm1_g.reshape(DEPTH, 1, D_MODEL)
    g2 = norm2_g.reshape(DEPTH, 1, D_MODEL)
    gq = jnp.tile(q_norm_g, (1, ATTN_Q_HEADS)).reshape(DEPTH, 1, QA_W)
    gk = jnp.tile(k_norm_g, (1, ATTN_KV_HEADS)).reshape(DEPTH, 1, KA_W)
    subg = _pad_segments(jnp.tile(subln_g, (1, DIFF_HEADS)), 1, DIFF_HEADS, DIFF_V_DIM, LANES
                         ).reshape(DEPTH, 1, VD_W)
    lam_in = jnp.stack([lambda_q1, lambda_k1, lambda_q2, lambda_k2], axis=1)
    w_r = jnp.concatenate([w_grp, w_exp, jnp.zeros((DEPTH, D_MODEL, LANES - N_ROUTE), F32)], axis=-1)
    wr_hi = w_r.astype(BF16)
    wr_lo = (w_r - wr_hi.astype(F32)).astype(BF16)
    br = jnp.concatenate([b_grp, b_exp, jnp.zeros((DEPTH, LANES - N_ROUTE), F32)], axis=-1
                         ).reshape(DEPTH, 1, LANES)
    seg = np.arange(QA_W) // HEAD_DIM
    bd64 = jnp.asarray((seg[:, None] == seg[None, :]).astype(np.float32) / HEAD_DIM, dtype=BF16)
    ltri = jnp.asarray(np.tril(np.ones((TM, TM), np.float32), -1), dtype=BF16)
    rope_tabs = (_rope_tables(DEC_SEQ, HEAD_DIM, HEAD_DIM, ATTN_Q_HEADS)
                 + _rope_tables(DEC_SEQ, DIFF_QK_DIM, DQK_SLOT, 2 * DIFF_HEADS))
    cs64 = _channel_dft()
    tw_p = _dft_tables(SEQ)
    tw_s = _dft_tables(DEC_SEQ)
    pka = cache_attn_k.reshape(DEC_BATCH, DEPTH, PAST_LEN, KA_W)
    pva = cache_attn_v.reshape(DEC_BATCH, DEPTH, PAST_LEN, KA_W)
    pkd = _pad_segments(cache_diff_k.reshape(DEC_BATCH, DEPTH, PAST_LEN, 2 * DIFF_HEADS * DIFF_QK_DIM),
                        3, 2 * DIFF_HEADS, DIFF_QK_DIM, DQK_SLOT)
    pvd = _pad_segments(cache_diff_v.reshape(DEC_BATCH, DEPTH, PAST_LEN, DIFF_HEADS * DIFF_V_DIM),
                        3, DIFF_HEADS, DIFF_V_DIM, LANES)
    prefix = [a.astype(BF16) for a in (pka, pva, pkd, pvd)]

    caches = []
    x_p = x_prompt.reshape(N_PROMPT, D_MODEL)
    x_s = x_sample.reshape(N_SAMPLE, D_MODEL)
    for layer in range(DEPTH):
        pre_p = _pre_attention(layer, x_p, mod5, g1, w_in_r, gq, gk, bd64, None, sample=False)
        pre_s = _pre_attention(layer, x_s, mod5, g1, w_in_r, gq, gk, bd64, rope_tabs, sample=True)
        caches.append(pre_p[8:])
        att_p = _attention(layer, pre_p, None, lam_in, subg, sample=False)
        att_s = _attention(layer, pre_s, prefix, lam_in, subg, sample=True)
        fo_p = _fourier(pre_p[7], cs64, tw_p, sample=False)
        fo_s = _fourier(pre_s[7], cs64, tw_s, sample=True)
        xn, h2, route, counts = _post_attention(layer, (x_p, x_s), (att_p, att_s), (fo_p, fo_s),
                                                woa, wof, mod5, g2, wr_hi, wr_lo, br, ltri)
        pos, te, tile, active, zero_tile = _sort_plan(route, counts)
        xs = _dispatch(zero_tile, pos, h2)
        ys = _moe(layer, te, tile, active, xs, w_gate, w_up, w_down)
        fg = final_g.reshape(1, D_MODEL) if layer + 1 == DEPTH else None
        x_p = _combine(layer, xn, ys, pos, route, mod5, fg, sample=False)
        x_s = _combine(layer, xn, ys, pos, route, mod5, fg, sample=True)
    y_prompt, y_sample = x_p, x_s

    def stack(i, shape, keep):
        arrs = [cl[i].reshape(shape)[..., :keep] for cl in caches]
        return jnp.stack(arrs, axis=1)

    new_attn_k = stack(0, (BATCH, SEQ, ATTN_KV_HEADS, HEAD_DIM), HEAD_DIM)
    new_attn_v = stack(1, (BATCH, SEQ, ATTN_KV_HEADS, HEAD_DIM), HEAD_DIM)
    new_diff_k = stack(2, (BATCH, SEQ, DIFF_HEADS, 2, DQK_SLOT), DIFF_QK_DIM)
    new_diff_v = stack(3, (BATCH, SEQ, DIFF_HEADS, LANES), DIFF_V_DIM)
    return (y_prompt.reshape(BATCH, SEQ, D_MODEL), y_sample.reshape(DEC_BATCH, DEC_SEQ, D_MODEL),
            new_attn_k, new_attn_v, new_diff_k, new_diff_v)
```

```python
import functools
import math

import jax
import jax.numpy as jnp
import numpy as np
from jax import lax
from jax.experimental import pallas as pl
from jax.experimental.pallas import tpu as pltpu

F32 = jnp.float32
BF16 = jnp.bfloat16

D_MODEL = 1024
BATCH = 32
SEQ = 256
DEPTH = 2
DEC_BATCH = 4
DEC_SEQ = 2048
PAST_LEN = 256
GRID_W = 64
HEAD_DIM = 64
ATTN_Q_HEADS = 6
ATTN_KV_HEADS = 2
DIFF_HEADS = 4
DIFF_QK_DIM = 48
DIFF_V_DIM = 96
FOURIER_GROUPS = 4
FOURIER_DIM = 64
N_GROUPS = 4
EXPERTS_PER_GROUP = 8
N_EXPERTS = N_GROUPS * EXPERTS_PER_GROUP
D_EXPERT = 512
ROPE_THETA = 10000.0
EPS = 1e-6
LOG2E = math.log2(math.e)
N_MOD = 6

LANES = 128
SUBLANES = 8

N_PROMPT = BATCH * SEQ
N_SAMPLE = DEC_BATCH * DEC_SEQ
N_TOK = N_PROMPT + N_SAMPLE
QA_W = ATTN_Q_HEADS * HEAD_DIM
QS_W = ATTN_Q_HEADS * LANES
KA_W = ATTN_KV_HEADS * HEAD_DIM
DQK_SLOT = 64
QD_W = DIFF_HEADS * 2 * DQK_SLOT
VD_W = DIFF_HEADS * LANES
UF_W = FOURIER_GROUPS * FOURIER_DIM
P_W = QA_W + 2 * KA_W + 2 * QD_W + VD_W + UF_W
ATT_W = QA_W + VD_W
MOD_ROWS = 8
N_ROUTE = N_GROUPS + N_EXPERTS

TM = 512
TMD = 512
TQ_SAMPLE = 256
TQF = 512
TMM = 512
N_ASSIGN = 2 * N_TOK
P_ROWS = N_ASSIGN + N_EXPERTS * TMM
N_MTILES = P_ROWS // TMM
VMEM_LIMIT = 48 * 1024 * 1024
NEG = -1e30


def _cparams(sem):
    return pltpu.CompilerParams(dimension_semantics=sem, vmem_limit_bytes=VMEM_LIMIT)


def _dot(a, b):
    return jnp.dot(a, b, preferred_element_type=F32)


def _dot_nt(a, b):
    return lax.dot_general(a, b, (((1,), (1,)), ((), ())), preferred_element_type=F32)


def _split(x):
    hi = x.astype(BF16)
    lo = (x - hi.astype(F32)).astype(BF16)
    return hi, lo


def _rms(x, g):
    ms = jnp.mean(x * x, axis=-1, keepdims=True)
    return x * lax.rsqrt(ms + EPS) * g


def _mod_kernel(cv_ref, w_ref, b_ref, o_ref):
    cv = cv_ref[...]
    s = cv / (1.0 + jnp.exp(-cv))
    s_hi, s_lo = _split(s)
    w_hi, w_lo = _split(w_ref[...])
    o_ref[...] = _dot(s_hi, w_hi) + _dot(s_hi, w_lo) + _dot(s_lo, w_hi) + b_ref[...]


def _modulation(cv, w_mod, b_mod):
    tn = 1536
    return pl.pallas_call(
        _mod_kernel,
        out_shape=jax.ShapeDtypeStruct((DEPTH, MOD_ROWS, N_MOD * D_MODEL), F32),
        grid=(DEPTH, N_MOD * D_MODEL // tn),
        in_specs=[
            pl.BlockSpec((MOD_ROWS, D_MODEL), lambda l, n: (0, 0)),
            pl.BlockSpec((None, D_MODEL, tn), lambda l, n: (l, 0, n)),
            pl.BlockSpec((None, 1, tn), lambda l, n: (l, 0, n)),
        ],
        out_specs=pl.BlockSpec((None, MOD_ROWS, tn), lambda l, n: (l, 0, n)),
        compiler_params=_cparams(("arbitrary", "arbitrary")),
        name="modulation",
    )(cv, w_mod, b_mod.reshape(DEPTH, 1, N_MOD * D_MODEL))


def _mod_spec(layer, chunk, row_fn):
    return pl.BlockSpec((None, None, None, 1, D_MODEL),
                        lambda i: (layer, row_fn(i), chunk, 0, 0))


def _merged_row(i):
    n_p = N_PROMPT // TM
    return jnp.where(i < n_p, 0, 1 + ((i - n_p) * TM) // DEC_SEQ)


def _segmean(x2, bd):
    hi, lo = _split(x2)
    return _dot(hi, bd) + _dot(lo, bd)


def _rope(x, c, s, n, slot):
    w = x.shape[1]
    pos = lax.broadcasted_iota(jnp.int32, x.shape, 1) % slot
    first = (pos < n) | ((pos >= 2 * n) & (pos < 3 * n))
    partner = jnp.where(first, -pltpu.roll(x, w - n, 1), pltpu.roll(x, n, 1))
    return x * c + partner * s


def _pre_kernel(*refs, rope, caches):
    x_ref, sh_ref, sc_ref, g1_ref, w_ref, gq_ref, gk_ref, bd_ref = refs[:8]
    pos = 8
    if rope:
        ca_ref, sa_ref, cd_ref, sd_ref = refs[pos:pos + 4]
        pos += 4
    qs_ref, ka_ref, va_ref, qd1_ref, qd2_ref, kd_ref, vd_ref, uf_ref = refs[pos:pos + 8]
    pos += 8
    if caches:
        cka_ref, cva_ref, ckd_ref, cvd_ref = refs[pos:pos + 4]

    x = x_ref[...]
    h = _rms(x, g1_ref[...]) * (1.0 + sc_ref[...]) + sh_ref[...]
    p = _dot(h.astype(BF16), w_ref[...])
    o = 0
    qa = p[:, o:o + QA_W]; o += QA_W
    ka = p[:, o:o + KA_W]; o += KA_W
    va = p[:, o:o + KA_W]; o += KA_W
    qd = p[:, o:o + QD_W]; o += QD_W
    kd = p[:, o:o + QD_W]; o += QD_W
    vd = p[:, o:o + VD_W]; o += VD_W
    uf = p[:, o:o + UF_W]

    bd = bd_ref[...]
    qa = qa * lax.rsqrt(_segmean(qa * qa, bd) + EPS) * gq_ref[...]
    ka = ka * lax.rsqrt(_segmean(ka * ka, bd[:KA_W, :KA_W]) + EPS) * gk_ref[...]
    if caches:
        cka_ref[...] = ka
        cva_ref[...] = va
        ckd_ref[...] = kd
        cvd_ref[...] = vd
    if rope:
        ca, sa, cd, sd = ca_ref[...], sa_ref[...], cd_ref[...], sd_ref[...]
        qa = _rope(qa, ca, sa, HEAD_DIM // 4, HEAD_DIM)
        ka = _rope(ka, ca[:, :KA_W], sa[:, :KA_W], HEAD_DIM // 4, HEAD_DIM)
        qd = _rope(qd, cd, sd, DIFF_QK_DIM // 4, DQK_SLOT)
        kd = _rope(kd, cd, sd, DIFF_QK_DIM // 4, DQK_SLOT)
    qa = qa * (HEAD_DIM ** -0.5 * LOG2E)
    qd = qd * (DIFF_QK_DIM ** -0.5 * LOG2E)
    nat_lo, nat_hi = _lane_halves(qa)
    left_lo, _ = _lane_halves(pltpu.roll(qa, QA_W - HEAD_DIM, 1))
    _, right_hi = _lane_halves(pltpu.roll(qa, HEAD_DIM, 1))
    cols = (nat_lo[:, 0:LANES], left_lo[:, 0:LANES], nat_lo[:, LANES:2 * LANES],
            nat_hi[:, LANES:2 * LANES], right_hi[:, 2 * LANES:], nat_hi[:, 2 * LANES:])
    for h, col in enumerate(cols):
        qs_ref[:, h * LANES:(h + 1) * LANES] = col.astype(BF16)
    qd_lo, qd_hi = _lane_halves(qd)
    qd1_ref[...] = qd_lo.astype(BF16)
    qd2_ref[...] = qd_hi.astype(BF16)
    ka_ref[...] = ka.astype(BF16)
    va_ref[...] = va.astype(BF16)
    kd_ref[...] = kd.astype(BF16)
    vd_ref[...] = vd.astype(BF16)
    uf_ref[...] = uf.astype(BF16)


def _lane_halves(x):
    lo = (lax.broadcasted_iota(jnp.int32, x.shape, 1) % LANES) < HEAD_DIM
    z = jnp.zeros_like(x)
    return jnp.where(lo, x, z), jnp.where(lo, z, x)


def _pre_attention(layer, x_half, mod5, g1, w_in_r, gq, gk, bd64, rope_tabs, *, sample):
    n_rows = N_SAMPLE if sample else N_PROMPT
    if sample:
        grid = (DEC_SEQ // TM, DEC_BATCH)
        blk = lambda p, b: b * (DEC_SEQ // TM) + p
        row_fn = lambda p, b: 1 + b
    else:
        grid = (n_rows // TM, 1)
        blk = lambda p, b: p
        row_fn = lambda p, b: 0
    mod_spec = lambda chunk: pl.BlockSpec((None, None, None, 1, D_MODEL),
                                          lambda p, b: (layer, row_fn(p, b), chunk, 0, 0))
    tile = lambda w: pl.BlockSpec((TM, w), lambda p, b: (blk(p, b), 0))
    lay3 = lambda a: pl.BlockSpec((None,) + a.shape[1:], lambda p, b: (layer, 0, 0))
    in_specs = [tile(D_MODEL), mod_spec(0), mod_spec(1), lay3(g1), lay3(w_in_r), lay3(gq), lay3(gk),
                pl.BlockSpec(bd64.shape, lambda p, b: (0, 0))]
    args = [x_half, mod5, mod5, g1, w_in_r, gq, gk, bd64]
    if sample:
        for t in rope_tabs:
            in_specs.append(pl.BlockSpec((TM, t.shape[1]), lambda p, b: (p, 0)))
            args.append(t)
    widths = (QS_W, KA_W, KA_W, QD_W, QD_W, QD_W, VD_W, UF_W)
    out_shape = [jax.ShapeDtypeStruct((n_rows, w), BF16) for w in widths]
    out_specs = [tile(w) for w in widths]
    if not sample:
        for w in (KA_W, KA_W, QD_W, VD_W):
            out_shape.append(jax.ShapeDtypeStruct((n_rows, w), F32))
            out_specs.append(tile(w))
    return pl.pallas_call(
        functools.partial(_pre_kernel, rope=sample, caches=not sample),
        out_shape=out_shape,
        grid=grid,
        in_specs=in_specs,
        out_specs=out_specs,
        compiler_params=_cparams(("arbitrary", "arbitrary")),
        name="pre_attention_sample" if sample else "pre_attention_prompt",
    )(*args)


def _softmax_pv(q_rows, k_pieces, v_pieces):
    s = [_dot_nt(q_rows, k) for k in k_pieces]
    m = s[0].max(axis=-1, keepdims=True)
    for si in s[1:]:
        m = jnp.maximum(m, si.max(axis=-1, keepdims=True))
    acc = None
    den = None
    for si, v in zip(s, v_pieces):
        e = jnp.exp2(si - m)
        d = e.sum(axis=-1, keepdims=True)
        o = _dot(e.astype(BF16), v)
        acc = o if acc is None else acc + o
        den = d if den is None else den + d
    return acc * (1.0 / den)


def _attn_kernel(*refs, n_pieces, lam_init):
    qs_ref, qd1_ref, qd2_ref = refs[:3]
    pos = 3
    pieces = []
    for _ in range(n_pieces):
        pieces.append(refs[pos:pos + 4])
        pos += 4
    lam_ref, subg_ref, o_ref = refs[pos:pos + 3]
    tq = o_ref.shape[0]

    lamv = lam_ref[...]
    s1 = jnp.sum(lamv[0:1] * lamv[1:2], axis=-1, keepdims=True)
    s2 = jnp.sum(lamv[2:3] * lamv[3:4], axis=-1, keepdims=True)
    lam = jnp.exp(s1) - jnp.exp(s2) + lam_init

    blocks = []
    for kv in range(ATTN_KV_HEADS):
        group = ATTN_Q_HEADS // ATTN_KV_HEADS
        q_rows = jnp.concatenate([qs_ref[:, (kv * group + g) * LANES:(kv * group + g + 1) * LANES]
                                  for g in range(group)], axis=0)
        o = _softmax_pv(q_rows, [pc[0][...] for pc in pieces], [pc[1][...] for pc in pieces])
        blocks.append([o[g * tq:(g + 1) * tq] for g in range(group)])
    lo = lax.broadcasted_iota(jnp.int32, (tq, LANES), 1) < HEAD_DIM
    swap = lambda x: pltpu.roll(x, HEAD_DIM, 1)
    b, c = blocks
    cols = (jnp.where(lo, b[0], swap(b[1])), jnp.where(lo, b[2], c[0]), jnp.where(lo, swap(c[1]), c[2]))
    for j, col in enumerate(cols):
        o_ref[:, j * LANES:(j + 1) * LANES] = col.astype(BF16)

    for h in range(DIFF_HEADS):
        sl = slice(h * LANES, (h + 1) * LANES)
        q_rows = jnp.concatenate([qd1_ref[:, sl], qd2_ref[:, sl]], axis=0)
        o = _softmax_pv(q_rows, [pc[2][:, sl] for pc in pieces], [pc[3][:, sl] for pc in pieces])
        od = o[:tq] - lam * o[tq:]
        ms = jnp.sum(od * od, axis=-1, keepdims=True) * (1.0 / DIFF_V_DIM)
        od = od * lax.rsqrt(ms + EPS) * subg_ref[:, sl] * (1.0 - lam_init)
        o_ref[:, QA_W + h * LANES:QA_W + (h + 1) * LANES] = od.astype(BF16)


def _attention(layer, pre, prefix, lam_in, subg, *, sample):
    qs, ka, va, qd1, qd2, kd, vd = pre[:7]
    lam_init = 0.8 - 0.6 * math.exp(-0.3 * layer)
    if sample:
        nb, seq, tq = DEC_BATCH, DEC_SEQ, TQ_SAMPLE
    else:
        nb, seq, tq = BATCH, SEQ, SEQ
    nq = seq // tq
    qspec = lambda w: pl.BlockSpec((tq, w), lambda b, q: (b * nq + q, 0))
    in_specs = [qspec(QS_W), qspec(QD_W), qspec(QD_W)]
    args = [qs, qd1, qd2]
    n_pieces = 1
    if sample:
        n_pieces = 2
        for a in prefix:
            in_specs.append(pl.BlockSpec((None, None) + a.shape[2:], lambda b, q: (b, layer, 0, 0)))
            args.append(a)
    for a, w in ((ka, KA_W), (va, KA_W), (kd, QD_W), (vd, VD_W)):
        in_specs.append(pl.BlockSpec((seq, w), lambda b, q: (b, 0)))
        args.append(a)
    in_specs += [pl.BlockSpec((None, 4, DIFF_QK_DIM), lambda b, q: (layer, 0, 0)),
                 pl.BlockSpec((None, 1, VD_W), lambda b, q: (layer, 0, 0))]
    args += [lam_in, subg]
    return pl.pallas_call(
        functools.partial(_attn_kernel, n_pieces=n_pieces, lam_init=lam_init),
        out_shape=jax.ShapeDtypeStruct((nb * seq, ATT_W), BF16),
        grid=(nb, nq),
        in_specs=in_specs,
        out_specs=pl.BlockSpec((tq, ATT_W), lambda b, q: (b * nq + q, 0)),
        compiler_params=_cparams(("arbitrary", "arbitrary")),
        name="attention_sample" if sample else "attention_prompt",
    )(*args)


def _fourier_kernel(u_ref, cs_ref, tw_ref, o_ref, ab_ref, *, seq, norm):
    @pl.when(pl.program_id(1) == 0)
    def _():
        ab = _dot(u_ref[...], cs_ref[...])
        ab_ref[0:seq, :] = ab[:, :UF_W].astype(BF16)
        ab_ref[seq:2 * seq, :] = ab[:, UF_W:].astype(BF16)

    o_ref[...] = (_dot(tw_ref[...], ab_ref[...]) * norm).astype(BF16)


def _fourier(uf, cs64, tw, *, sample):
    if sample:
        nb, seq, tq = DEC_BATCH, DEC_SEQ, TQF
    else:
        nb, seq, tq = BATCH, SEQ, SEQ
    nq = seq // tq
    in_specs = [pl.BlockSpec((seq, UF_W), lambda b, q: (b, 0)),
                pl.BlockSpec(cs64.shape, lambda b, q: (0, 0)),
                pl.BlockSpec((tq, 2 * seq), lambda b, q: (q, 0))]
    return pl.pallas_call(
        functools.partial(_fourier_kernel, seq=seq, norm=(seq * FOURIER_DIM) ** -0.5),
        out_shape=jax.ShapeDtypeStruct((nb * seq, UF_W), BF16),
        grid=(nb, nq),
        in_specs=in_specs,
        out_specs=pl.BlockSpec((tq, UF_W), lambda b, q: (b * nq + q, 0)),
        scratch_shapes=[pltpu.VMEM((2 * seq, UF_W), BF16)],
        compiler_params=_cparams(("arbitrary", "arbitrary")),
        name="fourier_sample" if sample else "fourier_prompt",
    )(uf, cs64, tw)


def _post_kernel(xp_ref, xs_ref, attp_ref, atts_ref, fop_ref, fos_ref, woa_ref, wof_ref,
                 gt1_ref, sh2_ref, sc2_ref, g2_ref, wr_ref, br_ref, ltri_ref,
                 xn_ref, h2_ref, route_ref, cnt_ref, carry_ref):
    @pl.when(pl.program_id(0) == 0)
    def _():
        carry_ref[...] = jnp.zeros_like(carry_ref)

    is_prompt = pl.program_id(0) < N_PROMPT // TM
    pick = lambda p_ref, s_ref: jnp.where(is_prompt, p_ref[...], s_ref[...])
    mix = (_dot(pick(attp_ref, atts_ref), woa_ref[...])
           + _dot(pick(fop_ref, fos_ref), wof_ref[...]))
    xn = pick(xp_ref, xs_ref) + gt1_ref[...] * mix
    xn_ref[...] = xn
    h2 = _rms(xn, g2_ref[...]) * (1.0 + sc2_ref[...]) + sh2_ref[...]
    h2_ref[...] = h2

    h_hi, h_lo = _split(h2)
    wr = wr_ref[...]
    hh = _dot(h_hi, wr)
    logits = hh[:, :LANES] + hh[:, LANES:] + _dot(h_lo, wr[:, :LANES]) + br_ref[...]
    lane = lax.broadcasted_iota(jnp.int32, logits.shape, 1).astype(F32)
    big = float(LANES)

    def first_max(v):
        m = v.max(axis=-1, keepdims=True)
        idx = jnp.where(v == m, lane, big).min(axis=-1, keepdims=True)
        return m, idx

    lg = jnp.where(lane < N_GROUPS, logits, NEG)
    mg, gi = first_max(lg)
    pg_top = 1.0 / jnp.exp(lg - mg).sum(axis=-1, keepdims=True)
    e_lo = N_GROUPS + gi * EXPERTS_PER_GROUP
    le = jnp.where((lane >= e_lo) & (lane < e_lo + EXPERTS_PER_GROUP), logits, NEG)
    m0, i0 = first_max(le)
    m1, i1 = first_max(jnp.where(lane == i0, NEG, le))
    e1 = jnp.exp(m1 - m0)
    inv = 1.0 / (1.0 + e1)
    w0 = pg_top * inv
    w1 = pg_top * (e1 * inv)

    hot0 = lane == i0
    hot1 = lane == i1
    onehot = jnp.where(hot0 | hot1, 1.0, 0.0)
    before = _dot(ltri_ref[...], onehot.astype(BF16)) + carry_ref[0:1, :]
    r0 = jnp.where(hot0, before, 0.0).sum(axis=-1, keepdims=True)
    r1 = jnp.where(hot1, before, 0.0).sum(axis=-1, keepdims=True)
    carry = carry_ref[...] + onehot.sum(axis=0, keepdims=True)
    carry_ref[...] = carry
    cnt_ref[...] = carry

    route = jnp.where(lane == 0, w0, 0.0)
    route = jnp.where(lane == 1, w1, route)
    route = jnp.where(lane == 2, i0 - N_GROUPS, route)
    route = jnp.where(lane == 3, i1 - N_GROUPS, route)
    route = jnp.where(lane == 4, r0, route)
    route = jnp.where(lane == 5, r1, route)
    route_ref[...] = route


def _post_attention(layer, x_pair, att_pair, fo_pair, woa, wof, mod5, g2, wr, br, ltri):
    n_p = N_PROMPT // TM
    tile = lambda w: pl.BlockSpec((TM, w), lambda i: (i, 0))
    ptile = lambda w: pl.BlockSpec((TM, w), lambda i: (jnp.minimum(i, n_p - 1), 0))
    stile = lambda w: pl.BlockSpec((TM, w), lambda i: (jnp.maximum(i - n_p, 0), 0))
    lay3 = lambda a: pl.BlockSpec((None,) + a.shape[1:], lambda i: (layer, 0, 0))
    return pl.pallas_call(
        _post_kernel,
        out_shape=[jax.ShapeDtypeStruct((N_TOK, D_MODEL), F32),
                   jax.ShapeDtypeStruct((N_TOK, D_MODEL), F32),
                   jax.ShapeDtypeStruct((N_TOK, LANES), F32),
                   jax.ShapeDtypeStruct((SUBLANES, LANES), F32)],
        grid=(N_TOK // TM,),
        in_specs=[ptile(D_MODEL), stile(D_MODEL), ptile(ATT_W), stile(ATT_W), ptile(UF_W), stile(UF_W),
                  lay3(woa), lay3(wof),
                  _mod_spec(layer, 2, _merged_row), _mod_spec(layer, 3, _merged_row),
                  _mod_spec(layer, 4, _merged_row), lay3(g2), lay3(wr), lay3(br),
                  pl.BlockSpec(ltri.shape, lambda i: (0, 0))],
        out_specs=[tile(D_MODEL), tile(D_MODEL), tile(LANES),
                   pl.BlockSpec((SUBLANES, LANES), lambda i: (0, 0))],
        scratch_shapes=[pltpu.VMEM((SUBLANES, LANES), F32)],
        compiler_params=_cparams(("arbitrary",)),
        name="post_attention_router",
    )(*x_pair, *att_pair, *fo_pair, woa, wof, mod5, mod5, mod5, g2, wr, br, ltri)


def _sort_plan(route, counts):
    cnt = counts[0, N_GROUPS:N_GROUPS + N_EXPERTS].astype(jnp.int32)
    padded = ((cnt + TMM - 1) // TMM) * TMM
    ends = jnp.cumsum(padded)
    starts = ends - padded
    ids = jnp.arange(N_EXPERTS, dtype=jnp.int32)
    expert = route[:, 2:4].astype(jnp.int32)
    rank = route[:, 4:6].astype(jnp.int32)
    start_of = jnp.sum(jnp.where(expert[..., None] == ids, starts, 0), axis=-1)
    pos = (start_of + rank).reshape(N_TOK // TMD, TMD, 2).transpose(0, 2, 1).reshape(N_TOK // TMD, 1, 2 * TMD)
    tile_start = jnp.arange(N_MTILES, dtype=jnp.int32) * TMM
    n_active = ends[-1] // TMM
    tile = jnp.minimum(jnp.arange(N_MTILES, dtype=jnp.int32), jnp.maximum(n_active - 1, 0))
    te = jnp.sum((ends[None, :] <= (tile * TMM)[:, None]).astype(jnp.int32), axis=1)
    te = jnp.minimum(te, N_EXPERTS - 1)
    active = (tile_start < ends[-1]).astype(jnp.int32)
    is_last = jnp.sum(((ends[None, :] - TMM) == tile_start[:, None]) & (padded[None, :] > 0), axis=1)
    zero_tile = ((is_last > 0) | (active == 0)).astype(jnp.int32)
    return pos, te, tile, active, zero_tile


def _dispatch_kernel(zero_ref, pos_ref, h2_ref, xs_hbm, zbuf_ref, sem):
    i = pl.program_id(0)

    @pl.when(i == 0)
    def _():
        zbuf_ref[...] = jnp.zeros_like(zbuf_ref)
        tile_copy = lambda j: pltpu.make_async_copy(zbuf_ref, xs_hbm.at[pl.ds(j * TMM, TMM)], sem.at[1])

        def zstart(j, c):
            @pl.when(zero_ref[j] == 1)
            def _():
                tile_copy(j).start()
            return c

        def zwait(j, c):
            @pl.when(zero_ref[j] == 1)
            def _():
                tile_copy(j).wait()
            return c

        lax.fori_loop(0, N_MTILES, zstart, 0)
        lax.fori_loop(0, N_MTILES, zwait, 0)

    def copy(g, k, dst_row):
        return pltpu.make_async_copy(h2_ref.at[g, pl.ds(k, 1)], xs_hbm.at[pl.ds(dst_row, 1)], sem.at[0])

    for s in range(2):
        _for_each_row(TMD, lambda a, g, k: copy(g, k, pos_ref[0, 0, s * TMD + a]).start(priority=k % 2))
    for s in range(2):
        _for_each_row(TMD, lambda a, g, k: copy(0, k, 0).wait())


def _for_each_row(n_rows, fn):
    def body(g, c):
        for k in range(SUBLANES):
            fn(g * SUBLANES + k, g, k)
        return c
    lax.fori_loop(0, n_rows // SUBLANES, body, 0)


def _dispatch(zero_tile, pos_tiles, h2):
    grid_spec = pltpu.PrefetchScalarGridSpec(
        num_scalar_prefetch=1,
        grid=(N_TOK // TMD,),
        in_specs=[pl.BlockSpec((1, 1, 2 * TMD), lambda i, z: (i, 0, 0), memory_space=pltpu.SMEM),
                  pl.BlockSpec((TMD // SUBLANES, SUBLANES, D_MODEL), lambda i, z: (i, 0, 0))],
        out_specs=pl.BlockSpec(memory_space=pl.ANY),
        scratch_shapes=[pltpu.VMEM((TMM, D_MODEL), F32), pltpu.SemaphoreType.DMA((2,))],
    )
    return pl.pallas_call(
        _dispatch_kernel,
        out_shape=jax.ShapeDtypeStruct((P_ROWS, D_MODEL), F32),
        grid_spec=grid_spec,
        compiler_params=_cparams(("arbitrary",)),
        name="moe_dispatch",
    )(zero_tile, pos_tiles, h2.reshape(N_TOK // SUBLANES, SUBLANES, D_MODEL))


def _moe_kernel(te_ref, tile_ref, act_ref, xs_ref, wg_ref, wu_ref, wd_ref, ys_ref,
                wgb_ref, wub_ref, wdb_ref):
    j = pl.program_id(0)
    prev = te_ref[jnp.maximum(j - 1, 0)]

    @pl.when((j == 0) | (te_ref[j] != prev))
    def _():
        wgb_ref[...] = wg_ref[...].astype(BF16)
        wub_ref[...] = wu_ref[...].astype(BF16)
        wdb_ref[...] = wd_ref[...].astype(BF16)

    @pl.when(act_ref[j] == 1)
    def _():
        x = xs_ref[...].astype(BF16)
        hg = _dot(x, wgb_ref[...])
        hu = _dot(x, wub_ref[...])
        a = (hg / (1.0 + jnp.exp(-hg))) * hu
        ys_ref[...] = _dot(a.astype(BF16), wdb_ref[...])

    @pl.when(act_ref[j] == 0)
    def _():
        ys_ref[...] = jnp.zeros_like(ys_ref)


def _moe(layer, te, tile, active, xs, w_gate, w_up, w_down):
    wspec = lambda shape: pl.BlockSpec((None, None) + shape, lambda j, te, tl, act: (layer, te[j], 0, 0))
    grid_spec = pltpu.PrefetchScalarGridSpec(
        num_scalar_prefetch=3,
        grid=(N_MTILES,),
        in_specs=[
            pl.BlockSpec((TMM, D_MODEL), lambda j, te, tl, act: (tl[j], 0)),
            wspec((D_MODEL, D_EXPERT)), wspec((D_MODEL, D_EXPERT)), wspec((D_EXPERT, D_MODEL)),
        ],
        out_specs=pl.BlockSpec((TMM, D_MODEL), lambda j, te, tl, act: (j, 0)),
        scratch_shapes=[pltpu.VMEM((D_MODEL, D_EXPERT), BF16), pltpu.VMEM((D_MODEL, D_EXPERT), BF16),
                        pltpu.VMEM((D_EXPERT, D_MODEL), BF16)],
    )
    return pl.pallas_call(
        _moe_kernel,
        out_shape=jax.ShapeDtypeStruct((P_ROWS, D_MODEL), F32),
        grid_spec=grid_spec,
        compiler_params=_cparams(("arbitrary",)),
        name="moe_grouped_matmul",
    )(te, tile, active, xs, w_gate, w_up, w_down)


def _combine_kernel(pos_ref, ys_hbm, xn_ref, route_ref, gt2_ref, *rest, final):
    o_ref, ybuf_ref, sem = rest[-3:]

    def copy(g, k, src_row):
        return pltpu.make_async_copy(ys_hbm.at[pl.ds(src_row, 1)], ybuf_ref.at[g, pl.ds(k, 1)], sem.at[0])

    _for_each_row(2 * TMD, lambda a, g, k: copy(g, k, pos_ref[0, 0, a]).start(priority=k % 2))
    _for_each_row(2 * TMD, lambda a, g, k: copy(g, k, 0).wait())

    r = route_ref[...]
    yb = ybuf_ref[...].reshape(2 * TMD, D_MODEL)
    y = r[:, 0:1] * yb[:TMD] + r[:, 1:2] * yb[TMD:]
    x = xn_ref[...] + gt2_ref[...] * y
    if final:
        x = _rms(x, rest[0][...])
    o_ref[...] = x


def _combine(layer, xn, ys, pos, route, mod5, final_g, *, sample):
    final = final_g is not None
    if sample:
        n_rows, off = N_SAMPLE, N_PROMPT // TMD
        row_fn = lambda i: 1 + (i * TMD) // DEC_SEQ
    else:
        n_rows, off, row_fn = N_PROMPT, 0, (lambda i: 0)
    tile = lambda w: pl.BlockSpec((TMD, w), lambda i: (i + off, 0))
    in_specs = [pl.BlockSpec((1, 1, 2 * TMD), lambda i: (i + off, 0, 0), memory_space=pltpu.SMEM),
                pl.BlockSpec(memory_space=pl.ANY),
                tile(D_MODEL), tile(LANES), _mod_spec(layer, 5, row_fn)]
    args = [pos, ys, xn, route, mod5]
    if final:
        in_specs.append(pl.BlockSpec((1, D_MODEL), lambda i: (0, 0)))
        args.append(final_g)
    return pl.pallas_call(
        functools.partial(_combine_kernel, final=final),
        out_shape=jax.ShapeDtypeStruct((n_rows, D_MODEL), F32),
        grid=(n_rows // TMD,),
        in_specs=in_specs,
        out_specs=pl.BlockSpec((TMD, D_MODEL), lambda i: (i, 0)),
        scratch_shapes=[pltpu.VMEM((2 * TMD // SUBLANES, SUBLANES, D_MODEL), F32),
                        pltpu.SemaphoreType.DMA((1,))],
        compiler_params=_cparams(("arbitrary",)),
        name="moe_combine_" + ("sample" if sample else "prompt") + ("_final" if final else ""),
    )(*args)


def _pad_segments(w, axis, nseg, seg, segp):
    shape = w.shape
    w = w.reshape(shape[:axis] + (nseg, seg) + shape[axis + 1:])
    pad = [(0, 0)] * w.ndim
    pad[axis + 1] = (0, segp - seg)
    w = jnp.pad(w, pad)
    return w.reshape(shape[:axis] + (nseg * segp,) + shape[axis + 1:])


def _axial_angles(seq, dim):
    rows = seq // GRID_W
    r = jnp.repeat(jnp.arange(rows), GRID_W).astype(F32)
    col = jnp.tile(jnp.arange(GRID_W), rows).astype(F32)
    n = dim // 4
    freqs = ROPE_THETA ** (-jnp.arange(n, dtype=F32) / n)
    return r[:, None] * freqs, col[:, None] * freqs


def _rope_tables(seq, dim, slot, nslots):
    ang_r, ang_c = _axial_angles(seq, dim)
    n = dim // 4
    z = jnp.zeros((seq, n), F32)
    pad = jnp.zeros((seq, slot - dim), F32)
    cr, sr, cc, sc = jnp.cos(ang_r), jnp.sin(ang_r), jnp.cos(ang_c), jnp.sin(ang_c)
    c = jnp.concatenate([cr, cr, cc, cc, pad], axis=1)
    s = jnp.concatenate([sr, sr, sc, sc, pad], axis=1)
    return [jnp.tile(t, (1, nslots)) for t in (c, s)]


def _dft_tables(seq):
    lo_n = 64
    t = jnp.arange(seq, dtype=jnp.int32)
    unit = 2.0 * math.pi / seq

    def cs(mult):
        ang = ((mult[:, None] * t[None, :]) % seq).astype(F32) * unit
        return jnp.cos(ang), jnp.sin(ang)

    ch, sh = cs(jnp.arange(seq // lo_n, dtype=jnp.int32) * lo_n)
    cl, sl = cs(jnp.arange(lo_n, dtype=jnp.int32))
    ch, sh, cl, sl = ch[:, None, :], sh[:, None, :], cl[None, :, :], sl[None, :, :]
    cos = (ch * cl - sh * sl).reshape(seq, seq)
    sin = (sh * cl + ch * sl).reshape(seq, seq)
    return jnp.concatenate([cos, -sin], axis=1).astype(BF16)


def _channel_dft():
    c = jnp.arange(FOURIER_DIM, dtype=jnp.int32)
    ang = ((c[:, None] * c[None, :]) % FOURIER_DIM).astype(F32) * (2.0 * math.pi / FOURIER_DIM)
    eye = jnp.eye(FOURIER_GROUPS, dtype=F32)
    return jnp.concatenate([jnp.kron(eye, jnp.cos(ang)), jnp.kron(eye, jnp.sin(ang))],
                           axis=1).astype(BF16)


def kernel(x_prompt, x_sample, cache_attn_k, cache_attn_v, cache_diff_k, cache_diff_v, c, c_ctx,
           norm1_g, w_mod, b_mod, w_in, w_out, q_norm_g, k_norm_g,
           lambda_q1, lambda_k1, lambda_q2, lambda_k2, subln_g, norm2_g,
           w_grp, b_grp, w_exp, b_exp, w_gate, w_up, w_down, final_g):
    cv = jnp.concatenate([c_ctx[None, :], c, jnp.zeros((MOD_ROWS - 1 - DEC_BATCH, D_MODEL), F32)], axis=0)
    mod5 = _modulation(cv, w_mod, b_mod).reshape(DEPTH, MOD_ROWS, N_MOD, 1, D_MODEL)

    o0, o1, o2, o3, o4 = QA_W + 2 * KA_W, QA_W + 2 * KA_W + 384, QA_W + 2 * KA_W + 768, 1792, 2048
    w_in_r = jnp.concatenate([
        w_in[..., :o0],
        _pad_segments(w_in[..., o0:o1], 2, 2 * DIFF_HEADS, DIFF_QK_DIM, DQK_SLOT),
        _pad_segments(w_in[..., o1:o2], 2, 2 * DIFF_HEADS, DIFF_QK_DIM, DQK_SLOT),
        _pad_segments(w_in[..., o2:o3], 2, DIFF_HEADS, DIFF_V_DIM, LANES),
        w_in[..., o3:o4]], axis=-1).astype(BF16)
    woa = jnp.concatenate([w_out[:, :QA_W],
                           _pad_segments(w_out[:, QA_W:QA_W + DIFF_HEADS * DIFF_V_DIM], 1,
                                         DIFF_HEADS, DIFF_V_DIM, LANES)], axis=1).astype(BF16)
    wof = w_out[:, QA_W + DIFF_HEADS * DIFF_V_DIM:].astype(BF16)
    g1 = norm1_g.reshape(DEPTH, 1, D_MODEL)
    g2 = norm2_g.reshape(DEPTH, 1, D_MODEL)
    gq = jnp.tile(q_norm_g, (1, ATTN_Q_HEADS)).reshape(DEPTH, 1, QA_W)
    gk = jnp.tile(k_norm_g, (1, ATTN_KV_HEADS)).reshape(DEPTH, 1, KA_W)
    subg = _pad_segments(jnp.tile(subln_g, (1, DIFF_HEADS)), 1, DIFF_HEADS, DIFF_V_DIM, LANES
                         ).reshape(DEPTH, 1, VD_W)
    lam_in = jnp.stack([lambda_q1, lambda_k1, lambda_q2, lambda_k2], axis=1)
    w_r = jnp.concatenate([w_grp, w_exp, jnp.zeros((DEPTH, D_MODEL, LANES - N_ROUTE), F32)], axis=-1)
    wr_hi = w_r.astype(BF16)
    wr = jnp.concatenate([wr_hi, (w_r - wr_hi.astype(F32)).astype(BF16)], axis=-1)
    br = jnp.concatenate([b_grp, b_exp, jnp.zeros((DEPTH, LANES - N_ROUTE), F32)], axis=-1
                         ).reshape(DEPTH, 1, LANES)
    seg = np.arange(QA_W) // HEAD_DIM
    bd64 = jnp.asarray((seg[:, None] == seg[None, :]).astype(np.float32) / HEAD_DIM, dtype=BF16)
    ltri = jnp.asarray(np.tril(np.ones((TM, TM), np.float32), -1), dtype=BF16)
    rope_tabs = (_rope_tables(DEC_SEQ, HEAD_DIM, HEAD_DIM, ATTN_Q_HEADS)
                 + _rope_tables(DEC_SEQ, DIFF_QK_DIM, DQK_SLOT, 2 * DIFF_HEADS))
    cs64 = _channel_dft()
    tw_p = _dft_tables(SEQ)
    tw_s = _dft_tables(DEC_SEQ)
    pka = cache_attn_k.reshape(DEC_BATCH, DEPTH, PAST_LEN, KA_W)
    pva = cache_attn_v.reshape(DEC_BATCH, DEPTH, PAST_LEN, KA_W)
    pkd = _pad_segments(cache_diff_k.reshape(DEC_BATCH, DEPTH, PAST_LEN, 2 * DIFF_HEADS * DIFF_QK_DIM),
                        3, 2 * DIFF_HEADS, DIFF_QK_DIM, DQK_SLOT)
    pvd = _pad_segments(cache_diff_v.reshape(DEC_BATCH, DEPTH, PAST_LEN, DIFF_HEADS * DIFF_V_DIM),
                        3, DIFF_HEADS, DIFF_V_DIM, LANES)
    prefix = [a.astype(BF16) for a in (pka, pva, pkd, pvd)]

    caches = []
    x_p = x_prompt.reshape(N_PROMPT, D_MODEL)
    x_s = x_sample.reshape(N_SAMPLE, D_MODEL)
    for layer in range(DEPTH):
        pre_p = _pre_attention(layer, x_p, mod5, g1, w_in_r, gq, gk, bd64, None, sample=False)
        pre_s = _pre_attention(layer, x_s, mod5, g1, w_in_r, gq, gk, bd64, rope_tabs, sample=True)
        caches.append(pre_p[8:])
        att_p = _attention(layer, pre_p, None, lam_in, subg, sample=False)
        att_s = _attention(layer, pre_s, prefix, lam_in, subg, sample=True)
        fo_p = _fourier(pre_p[7], cs64, tw_p, sample=False)
        fo_s = _fourier(pre_s[7], cs64, tw_s, sample=True)
        xn, h2, route, counts = _post_attention(layer, (x_p, x_s), (att_p, att_s), (fo_p, fo_s),
                                                woa, wof, mod5, g2, wr, br, ltri)
        pos, te, tile, active, zero_tile = _sort_plan(route, counts)
        xs = _dispatch(zero_tile, pos, h2)
        ys = _moe(layer, te, tile, active, xs, w_gate, w_up, w_down)
        fg = final_g.reshape(1, D_MODEL) if layer + 1 == DEPTH else None
        x_p = _combine(layer, xn, ys, pos, route, mod5, fg, sample=False)
        x_s = _combine(layer, xn, ys, pos, route, mod5, fg, sample=True)
    y_prompt, y_sample = x_p, x_s

    def stack(i, shape, keep):
        arrs = [cl[i].reshape(shape)[..., :keep] for cl in caches]
        return jnp.stack(arrs, axis=1)

    new_attn_k = stack(0, (BATCH, SEQ, ATTN_KV_HEADS, HEAD_DIM), HEAD_DIM)
    new_attn_v = stack(1, (BATCH, SEQ, ATTN_KV_HEADS, HEAD_DIM), HEAD_DIM)
    new_diff_k = stack(2, (BATCH, SEQ, DIFF_HEADS, 2, DQK_SLOT), DIFF_QK_DIM)
    new_diff_v = stack(3, (BATCH, SEQ, DIFF_HEADS, LANES), DIFF_V_DIM)
    return (y_prompt.reshape(BATCH, SEQ, D_MODEL), y_sample.reshape(DEC_BATCH, DEC_SEQ, D_MODEL),
            new_attn_k, new_attn_v, new_diff_k, new_diff_v)
```

```python
import functools
import math

import jax
import jax.numpy as jnp
import numpy as np
from jax import lax
from jax.experimental import pallas as pl
from jax.experimental.pallas import tpu as pltpu

F32 = jnp.float32
BF16 = jnp.bfloat16

D_MODEL = 1024
BATCH = 32
SEQ = 256
DEPTH = 2
DEC_BATCH = 4
DEC_SEQ = 2048
PAST_LEN = 256
GRID_W = 64
HEAD_DIM = 64
ATTN_Q_HEADS = 6
ATTN_KV_HEADS = 2
DIFF_HEADS = 4
DIFF_QK_DIM = 48
DIFF_V_DIM = 96
FOURIER_GROUPS = 4
FOURIER_DIM = 64
N_GROUPS = 4
EXPERTS_PER_GROUP = 8
N_EXPERTS = N_GROUPS * EXPERTS_PER_GROUP
D_EXPERT = 512
ROPE_THETA = 10000.0
EPS = 1e-6
LOG2E = math.log2(math.e)
N_MOD = 6

LANES = 128
SUBLANES = 8

N_PROMPT = BATCH * SEQ
N_SAMPLE = DEC_BATCH * DEC_SEQ
N_TOK = N_PROMPT + N_SAMPLE
QA_W = ATTN_Q_HEADS * HEAD_DIM
QS_W = ATTN_Q_HEADS * LANES
KA_W = ATTN_KV_HEADS * HEAD_DIM
DQK_SLOT = 64
QD_W = DIFF_HEADS * 2 * DQK_SLOT
VD_W = DIFF_HEADS * LANES
UF_W = FOURIER_GROUPS * FOURIER_DIM
P_W = QA_W + 2 * KA_W + 2 * QD_W + VD_W + UF_W
ATT_W = QA_W + VD_W
MOD_ROWS = 8
N_ROUTE = N_GROUPS + N_EXPERTS

TM = 512
TMD = 512
TQ_SAMPLE = 256
TQF = 1024
TMM = 512
N_ASSIGN = 2 * N_TOK
P_ROWS = N_ASSIGN + N_EXPERTS * TMM
N_MTILES = P_ROWS // TMM
VMEM_LIMIT = 48 * 1024 * 1024
NEG = -1e30


def _cparams(sem):
    return pltpu.CompilerParams(dimension_semantics=sem, vmem_limit_bytes=VMEM_LIMIT)


def _dot(a, b):
    return jnp.dot(a, b, preferred_element_type=F32)


def _dot_nt(a, b):
    return lax.dot_general(a, b, (((1,), (1,)), ((), ())), preferred_element_type=F32)


def _split(x):
    hi = x.astype(BF16)
    lo = (x - hi.astype(F32)).astype(BF16)
    return hi, lo


def _rms(x, g):
    ms = jnp.mean(x * x, axis=-1, keepdims=True)
    return x * lax.rsqrt(ms + EPS) * g


def _mod_kernel(cv_ref, w_ref, b_ref, o_ref):
    cv = cv_ref[...]
    s = cv / (1.0 + jnp.exp(-cv))
    s_hi, s_lo = _split(s)
    w_hi, w_lo = _split(w_ref[...])
    o_ref[...] = _dot(s_hi, w_hi) + _dot(s_hi, w_lo) + _dot(s_lo, w_hi) + b_ref[...]


def _modulation(cv, w_mod, b_mod):
    tn = 1536
    return pl.pallas_call(
        _mod_kernel,
        out_shape=jax.ShapeDtypeStruct((DEPTH, MOD_ROWS, N_MOD * D_MODEL), F32),
        grid=(DEPTH, N_MOD * D_MODEL // tn),
        in_specs=[
            pl.BlockSpec((MOD_ROWS, D_MODEL), lambda l, n: (0, 0)),
            pl.BlockSpec((None, D_MODEL, tn), lambda l, n: (l, 0, n)),
            pl.BlockSpec((None, 1, tn), lambda l, n: (l, 0, n)),
        ],
        out_specs=pl.BlockSpec((None, MOD_ROWS, tn), lambda l, n: (l, 0, n)),
        compiler_params=_cparams(("arbitrary", "arbitrary")),
        name="modulation",
    )(cv, w_mod, b_mod.reshape(DEPTH, 1, N_MOD * D_MODEL))


def _mod_spec(layer, chunk, row_fn):
    return pl.BlockSpec((None, None, None, 1, D_MODEL),
                        lambda i: (layer, row_fn(i), chunk, 0, 0))


def _merged_row(i):
    n_p = N_PROMPT // TM
    return jnp.where(i < n_p, 0, 1 + ((i - n_p) * TM) // DEC_SEQ)


def _segmean(x2, bd):
    hi, lo = _split(x2)
    return _dot(hi, bd) + _dot(lo, bd)


def _rope(x, c, s, n, slot):
    w = x.shape[1]
    pos = lax.broadcasted_iota(jnp.int32, x.shape, 1) % slot
    first = (pos < n) | ((pos >= 2 * n) & (pos < 3 * n))
    partner = jnp.where(first, -pltpu.roll(x, w - n, 1), pltpu.roll(x, n, 1))
    return x * c + partner * s


def _pre_kernel(*refs, rope, caches):
    x_ref, sh_ref, sc_ref, g1_ref, w_ref, gq_ref, gk_ref, bd_ref = refs[:8]
    pos = 8
    if rope:
        ca_ref, sa_ref, cd_ref, sd_ref = refs[pos:pos + 4]
        pos += 4
    qs_ref, ka_ref, va_ref, qd1_ref, qd2_ref, kd_ref, vd_ref, uf_ref = refs[pos:pos + 8]
    pos += 8
    if caches:
        cka_ref, cva_ref, ckd_ref, cvd_ref = refs[pos:pos + 4]

    x = x_ref[...]
    h = _rms(x, g1_ref[...]) * (1.0 + sc_ref[...]) + sh_ref[...]
    p = _dot(h.astype(BF16), w_ref[...])
    o = 0
    qa = p[:, o:o + QA_W]; o += QA_W
    ka = p[:, o:o + KA_W]; o += KA_W
    va = p[:, o:o + KA_W]; o += KA_W
    qd = p[:, o:o + QD_W]; o += QD_W
    kd = p[:, o:o + QD_W]; o += QD_W
    vd = p[:, o:o + VD_W]; o += VD_W
    uf = p[:, o:o + UF_W]

    bd = bd_ref[...]
    qa = qa * lax.rsqrt(_segmean(qa * qa, bd) + EPS) * gq_ref[...]
    ka = ka * lax.rsqrt(_segmean(ka * ka, bd[:KA_W, :KA_W]) + EPS) * gk_ref[...]
    if caches:
        cka_ref[...] = ka
        cva_ref[...] = va
        ckd_ref[...] = kd
        cvd_ref[...] = vd
    if rope:
        ca, sa, cd, sd = ca_ref[...], sa_ref[...], cd_ref[...], sd_ref[...]
        qa = _rope(qa, ca, sa, HEAD_DIM // 4, HEAD_DIM)
        ka = _rope(ka, ca[:, :KA_W], sa[:, :KA_W], HEAD_DIM // 4, HEAD_DIM)
        qd = _rope(qd, cd, sd, DIFF_QK_DIM // 4, DQK_SLOT)
        kd = _rope(kd, cd, sd, DIFF_QK_DIM // 4, DQK_SLOT)
    qa = qa * (HEAD_DIM ** -0.5 * LOG2E)
    qd = qd * (DIFF_QK_DIM ** -0.5 * LOG2E)
    nat_lo, nat_hi = _lane_halves(qa)
    left_lo, _ = _lane_halves(pltpu.roll(qa, QA_W - HEAD_DIM, 1))
    _, right_hi = _lane_halves(pltpu.roll(qa, HEAD_DIM, 1))
    cols = (nat_lo[:, 0:LANES], left_lo[:, 0:LANES], nat_lo[:, LANES:2 * LANES],
            nat_hi[:, LANES:2 * LANES], right_hi[:, 2 * LANES:], nat_hi[:, 2 * LANES:])
    for h, col in enumerate(cols):
        qs_ref[:, h * LANES:(h + 1) * LANES] = col.astype(BF16)
    qd_lo, qd_hi = _lane_halves(qd)
    qd1_ref[...] = qd_lo.astype(BF16)
    qd2_ref[...] = qd_hi.astype(BF16)
    ka_ref[...] = ka.astype(BF16)
    va_ref[...] = va.astype(BF16)
    kd_ref[...] = kd.astype(BF16)
    vd_ref[...] = vd.astype(BF16)
    uf_ref[...] = uf.astype(BF16)


def _lane_halves(x):
    lo = (lax.broadcasted_iota(jnp.int32, x.shape, 1) % LANES) < HEAD_DIM
    z = jnp.zeros_like(x)
    return jnp.where(lo, x, z), jnp.where(lo, z, x)


def _pre_attention(layer, x_half, mod5, g1, w_in_r, gq, gk, bd64, rope_tabs, *, sample):
    n_rows = N_SAMPLE if sample else N_PROMPT
    if sample:
        grid = (DEC_SEQ // TM, DEC_BATCH)
        blk = lambda p, b: b * (DEC_SEQ // TM) + p
        row_fn = lambda p, b: 1 + b
    else:
        grid = (n_rows // TM, 1)
        blk = lambda p, b: p
        row_fn = lambda p, b: 0
    mod_spec = lambda chunk: pl.BlockSpec((None, None, None, 1, D_MODEL),
                                          lambda p, b: (layer, row_fn(p, b), chunk, 0, 0))
    tile = lambda w: pl.BlockSpec((TM, w), lambda p, b: (blk(p, b), 0))
    lay3 = lambda a: pl.BlockSpec((None,) + a.shape[1:], lambda p, b: (layer, 0, 0))
    in_specs = [tile(D_MODEL), mod_spec(0), mod_spec(1), lay3(g1), lay3(w_in_r), lay3(gq), lay3(gk),
                pl.BlockSpec(bd64.shape, lambda p, b: (0, 0))]
    args = [x_half, mod5, mod5, g1, w_in_r, gq, gk, bd64]
    if sample:
        for t in rope_tabs:
            in_specs.append(pl.BlockSpec((TM, t.shape[1]), lambda p, b: (p, 0)))
            args.append(t)
    widths = (QS_W, KA_W, KA_W, QD_W, QD_W, QD_W, VD_W, UF_W)
    out_shape = [jax.ShapeDtypeStruct((n_rows, w), BF16) for w in widths]
    out_specs = [tile(w) for w in widths]
    if not sample:
        for w in (KA_W, KA_W, QD_W, VD_W):
            out_shape.append(jax.ShapeDtypeStruct((n_rows, w), F32))
            out_specs.append(tile(w))
    return pl.pallas_call(
        functools.partial(_pre_kernel, rope=sample, caches=not sample),
        out_shape=out_shape,
        grid=grid,
        in_specs=in_specs,
        out_specs=out_specs,
        compiler_params=_cparams(("arbitrary", "arbitrary")),
        name="pre_attention_sample" if sample else "pre_attention_prompt",
    )(*args)


def _softmax_pv(q_rows, k_pieces, v_pieces):
    s = [_dot_nt(q_rows, k) for k in k_pieces]
    m = s[0].max(axis=-1, keepdims=True)
    for si in s[1:]:
        m = jnp.maximum(m, si.max(axis=-1, keepdims=True))
    acc = None
    den = None
    for si, v in zip(s, v_pieces):
        e = jnp.exp2(si - m)
        d = e.sum(axis=-1, keepdims=True)
        o = _dot(e.astype(BF16), v)
        acc = o if acc is None else acc + o
        den = d if den is None else den + d
    return acc * (1.0 / den)


def _attn_kernel(*refs, n_pieces, lam_init):
    qs_ref, qd1_ref, qd2_ref = refs[:3]
    pos = 3
    pieces = []
    for _ in range(n_pieces):
        pieces.append(refs[pos:pos + 4])
        pos += 4
    lam_ref, subg_ref, o_ref = refs[pos:pos + 3]
    tq = o_ref.shape[0]

    lamv = lam_ref[...]
    s1 = jnp.sum(lamv[0:1] * lamv[1:2], axis=-1, keepdims=True)
    s2 = jnp.sum(lamv[2:3] * lamv[3:4], axis=-1, keepdims=True)
    lam = jnp.exp(s1) - jnp.exp(s2) + lam_init

    blocks = []
    for kv in range(ATTN_KV_HEADS):
        group = ATTN_Q_HEADS // ATTN_KV_HEADS
        q_rows = jnp.concatenate([qs_ref[:, (kv * group + g) * LANES:(kv * group + g + 1) * LANES]
                                  for g in range(group)], axis=0)
        o = _softmax_pv(q_rows, [pc[0][...] for pc in pieces], [pc[1][...] for pc in pieces])
        blocks.append([o[g * tq:(g + 1) * tq] for g in range(group)])
    lo = lax.broadcasted_iota(jnp.int32, (tq, LANES), 1) < HEAD_DIM
    swap = lambda x: pltpu.roll(x, HEAD_DIM, 1)
    b, c = blocks
    cols = (jnp.where(lo, b[0], swap(b[1])), jnp.where(lo, b[2], c[0]), jnp.where(lo, swap(c[1]), c[2]))
    for j, col in enumerate(cols):
        o_ref[:, j * LANES:(j + 1) * LANES] = col.astype(BF16)

    for h in range(DIFF_HEADS):
        sl = slice(h * LANES, (h + 1) * LANES)
        q_rows = jnp.concatenate([qd1_ref[:, sl], qd2_ref[:, sl]], axis=0)
        o = _softmax_pv(q_rows, [pc[2][:, sl] for pc in pieces], [pc[3][:, sl] for pc in pieces])
        od = o[:tq] - lam * o[tq:]
        ms = jnp.sum(od * od, axis=-1, keepdims=True) * (1.0 / DIFF_V_DIM)
        od = od * lax.rsqrt(ms + EPS) * subg_ref[:, sl] * (1.0 - lam_init)
        o_ref[:, QA_W + h * LANES:QA_W + (h + 1) * LANES] = od.astype(BF16)


def _attention(layer, pre, prefix, lam_in, subg, *, sample):
    qs, ka, va, qd1, qd2, kd, vd = pre[:7]
    lam_init = 0.8 - 0.6 * math.exp(-0.3 * layer)
    if sample:
        nb, seq, tq = DEC_BATCH, DEC_SEQ, TQ_SAMPLE
    else:
        nb, seq, tq = BATCH, SEQ, SEQ
    nq = seq // tq
    qspec = lambda w: pl.BlockSpec((tq, w), lambda b, q: (b * nq + q, 0))
    in_specs = [qspec(QS_W), qspec(QD_W), qspec(QD_W)]
    args = [qs, qd1, qd2]
    n_pieces = 1
    if sample:
        n_pieces = 2
        for a in prefix:
            in_specs.append(pl.BlockSpec((None, None) + a.shape[2:], lambda b, q: (b, layer, 0, 0)))
            args.append(a)
    for a, w in ((ka, KA_W), (va, KA_W), (kd, QD_W), (vd, VD_W)):
        in_specs.append(pl.BlockSpec((seq, w), lambda b, q: (b, 0)))
        args.append(a)
    in_specs += [pl.BlockSpec((None, 4, DIFF_QK_DIM), lambda b, q: (layer, 0, 0)),
                 pl.BlockSpec((None, 1, VD_W), lambda b, q: (layer, 0, 0))]
    args += [lam_in, subg]
    return pl.pallas_call(
        functools.partial(_attn_kernel, n_pieces=n_pieces, lam_init=lam_init),
        out_shape=jax.ShapeDtypeStruct((nb * seq, ATT_W), BF16),
        grid=(nb, nq),
        in_specs=in_specs,
        out_specs=pl.BlockSpec((tq, ATT_W), lambda b, q: (b * nq + q, 0)),
        compiler_params=_cparams(("arbitrary", "arbitrary")),
        name="attention_sample" if sample else "attention_prompt",
    )(*args)


def _fourier_kernel(u_ref, cs_ref, tw_ref, o_ref, ab_ref, *, seq, norm):
    @pl.when(pl.program_id(1) == 0)
    def _():
        ab = _dot(u_ref[...], cs_ref[...])
        ab_ref[0:seq, :] = ab[:, :UF_W].astype(BF16)
        ab_ref[seq:2 * seq, :] = ab[:, UF_W:].astype(BF16)

    o_ref[...] = (_dot(tw_ref[...], ab_ref[...]) * norm).astype(BF16)


def _fourier(uf, cs64, tw, *, sample):
    if sample:
        nb, seq, tq = DEC_BATCH, DEC_SEQ, TQF
    else:
        nb, seq, tq = BATCH, SEQ, SEQ
    nq = seq // tq
    in_specs = [pl.BlockSpec((seq, UF_W), lambda b, q: (b, 0)),
                pl.BlockSpec(cs64.shape, lambda b, q: (0, 0)),
                pl.BlockSpec((tq, 2 * seq), lambda b, q: (q, 0))]
    return pl.pallas_call(
        functools.partial(_fourier_kernel, seq=seq, norm=(seq * FOURIER_DIM) ** -0.5),
        out_shape=jax.ShapeDtypeStruct((nb * seq, UF_W), BF16),
        grid=(nb, nq),
        in_specs=in_specs,
        out_specs=pl.BlockSpec((tq, UF_W), lambda b, q: (b * nq + q, 0)),
        scratch_shapes=[pltpu.VMEM((2 * seq, UF_W), BF16)],
        compiler_params=_cparams(("arbitrary", "arbitrary")),
        name="fourier_sample" if sample else "fourier_prompt",
    )(uf, cs64, tw)


def _post_kernel(xp_ref, xs_ref, attp_ref, atts_ref, fop_ref, fos_ref, woa_ref, wof_ref,
                 gt1_ref, sh2_ref, sc2_ref, g2_ref, wr_ref, br_ref, ltri_ref,
                 xn_ref, h2_ref, route_ref, cnt_ref, routet_ref, carry_ref):
    @pl.when(pl.program_id(0) == 0)
    def _():
        carry_ref[...] = jnp.zeros_like(carry_ref)

    is_prompt = pl.program_id(0) < N_PROMPT // TM
    pick = lambda p_ref, s_ref: jnp.where(is_prompt, p_ref[...], s_ref[...])
    mix = (_dot(pick(attp_ref, atts_ref), woa_ref[...])
           + _dot(pick(fop_ref, fos_ref), wof_ref[...]))
    xn = pick(xp_ref, xs_ref) + gt1_ref[...] * mix
    xn_ref[...] = xn
    h2 = _rms(xn, g2_ref[...]) * (1.0 + sc2_ref[...]) + sh2_ref[...]
    h2_ref[...] = h2

    h_hi, h_lo = _split(h2)
    wr = wr_ref[...]
    hh = _dot(h_hi, wr)
    logits = hh[:, :LANES] + hh[:, LANES:] + _dot(h_lo, wr[:, :LANES]) + br_ref[...]
    lane = lax.broadcasted_iota(jnp.int32, logits.shape, 1).astype(F32)
    big = float(LANES)

    def first_max(v):
        m = v.max(axis=-1, keepdims=True)
        idx = jnp.where(v == m, lane, big).min(axis=-1, keepdims=True)
        return m, idx

    lg = jnp.where(lane < N_GROUPS, logits, NEG)
    mg, gi = first_max(lg)
    pg_top = 1.0 / jnp.exp(lg - mg).sum(axis=-1, keepdims=True)
    e_lo = N_GROUPS + gi * EXPERTS_PER_GROUP
    le = jnp.where((lane >= e_lo) & (lane < e_lo + EXPERTS_PER_GROUP), logits, NEG)
    m0, i0 = first_max(le)
    m1, i1 = first_max(jnp.where(lane == i0, NEG, le))
    e1 = jnp.exp(m1 - m0)
    inv = 1.0 / (1.0 + e1)
    w0 = pg_top * inv
    w1 = pg_top * (e1 * inv)

    hot0 = lane == i0
    hot1 = lane == i1
    onehot = jnp.where(hot0 | hot1, 1.0, 0.0)
    before = _dot(ltri_ref[...], onehot.astype(BF16)) + carry_ref[0:1, :]
    r0 = jnp.where(hot0, before, 0.0).sum(axis=-1, keepdims=True)
    r1 = jnp.where(hot1, before, 0.0).sum(axis=-1, keepdims=True)
    carry = carry_ref[...] + onehot.sum(axis=0, keepdims=True)
    carry_ref[...] = carry
    cnt_ref[...] = carry

    route = jnp.where(lane == 0, w0, 0.0)
    route = jnp.where(lane == 1, w1, route)
    route = jnp.where(lane == 2, i0 - N_GROUPS, route)
    route = jnp.where(lane == 3, i1 - N_GROUPS, route)
    route = jnp.where(lane == 4, r0, route)
    route = jnp.where(lane == 5, r1, route)
    route_ref[...] = route
    routet_ref[...] = route.T[:SUBLANES]


def _post_attention(layer, x_pair, att_pair, fo_pair, woa, wof, mod5, g2, wr, br, ltri):
    n_p = N_PROMPT // TM
    tile = lambda w: pl.BlockSpec((TM, w), lambda i: (i, 0))
    ptile = lambda w: pl.BlockSpec((TM, w), lambda i: (jnp.minimum(i, n_p - 1), 0))
    stile = lambda w: pl.BlockSpec((TM, w), lambda i: (jnp.maximum(i - n_p, 0), 0))
    lay3 = lambda a: pl.BlockSpec((None,) + a.shape[1:], lambda i: (layer, 0, 0))
    return pl.pallas_call(
        _post_kernel,
        out_shape=[jax.ShapeDtypeStruct((N_TOK, D_MODEL), F32),
                   jax.ShapeDtypeStruct((N_TOK, D_MODEL), F32),
                   jax.ShapeDtypeStruct((N_TOK, LANES), F32),
                   jax.ShapeDtypeStruct((SUBLANES, LANES), F32),
                   jax.ShapeDtypeStruct((N_TOK // TM, SUBLANES, TM), F32)],
        grid=(N_TOK // TM,),
        in_specs=[ptile(D_MODEL), stile(D_MODEL), ptile(ATT_W), stile(ATT_W), ptile(UF_W), stile(UF_W),
                  lay3(woa), lay3(wof),
                  _mod_spec(layer, 2, _merged_row), _mod_spec(layer, 3, _merged_row),
                  _mod_spec(layer, 4, _merged_row), lay3(g2), lay3(wr), lay3(br),
                  pl.BlockSpec(ltri.shape, lambda i: (0, 0))],
        out_specs=[tile(D_MODEL), tile(D_MODEL), tile(LANES),
                   pl.BlockSpec((SUBLANES, LANES), lambda i: (0, 0)),
                   pl.BlockSpec((None, SUBLANES, TM), lambda i: (i, 0, 0))],
        scratch_shapes=[pltpu.VMEM((SUBLANES, LANES), F32)],
        compiler_params=_cparams(("arbitrary",)),
        name="post_attention_router",
    )(*x_pair, *att_pair, *fo_pair, woa, wof, mod5, mod5, mod5, g2, wr, br, ltri)


def _sort_plan(route_t, counts):
    assert TMD == TM
    cnt = counts[0, N_GROUPS:N_GROUPS + N_EXPERTS].astype(jnp.int32)
    padded = ((cnt + TMM - 1) // TMM) * TMM
    ends = jnp.cumsum(padded)
    starts = ends - padded
    expert = route_t[:, 2:4, :].astype(jnp.int32)
    rank = route_t[:, 4:6, :].astype(jnp.int32)
    start_of = functools.reduce(
        lambda acc, k: acc + jnp.where(expert == k, starts[k], 0), range(N_EXPERTS), jnp.zeros_like(expert))
    pos = (start_of + rank).reshape(N_TOK // TMD, 1, 2 * TMD)
    tile_start = jnp.arange(N_MTILES, dtype=jnp.int32) * TMM
    n_active = ends[-1] // TMM
    tile = jnp.minimum(jnp.arange(N_MTILES, dtype=jnp.int32), jnp.maximum(n_active - 1, 0))
    te = jnp.sum((ends[None, :] <= (tile * TMM)[:, None]).astype(jnp.int32), axis=1)
    te = jnp.minimum(te, N_EXPERTS - 1)
    active = (tile_start < ends[-1]).astype(jnp.int32)
    is_last = jnp.sum(((ends[None, :] - TMM) == tile_start[:, None]) & (padded[None, :] > 0), axis=1)
    zero_tile = ((is_last > 0) | (active == 0)).astype(jnp.int32)
    return pos, te, tile, active, zero_tile


def _dispatch_kernel(zero_ref, pos_ref, h2_ref, xs_hbm, zbuf_ref, sem):
    i = pl.program_id(0)

    @pl.when(i == 0)
    def _():
        zbuf_ref[...] = jnp.zeros_like(zbuf_ref)
        tile_copy = lambda j: pltpu.make_async_copy(zbuf_ref, xs_hbm.at[pl.ds(j * TMM, TMM)], sem.at[1])

        def zstart(j, c):
            @pl.when(zero_ref[j] == 1)
            def _():
                tile_copy(j).start()
            return c

        def zwait(j, c):
            @pl.when(zero_ref[j] == 1)
            def _():
                tile_copy(j).wait()
            return c

        lax.fori_loop(0, N_MTILES, zstart, 0)
        lax.fori_loop(0, N_MTILES, zwait, 0)

    def copy(g, k, dst_row):
        return pltpu.make_async_copy(h2_ref.at[g, pl.ds(k, 1)], xs_hbm.at[pl.ds(dst_row, 1)], sem.at[0])

    for s in range(2):
        _for_each_row(TMD, lambda a, g, k: copy(g, k, pos_ref[0, 0, s * TMD + a]).start(priority=k % 2))
    for s in range(2):
        _for_each_row(TMD, lambda a, g, k: copy(0, k, 0).wait())


def _for_each_row(n_rows, fn):
    def body(g, c):
        for k in range(SUBLANES):
            fn(g * SUBLANES + k, g, k)
        return c
    lax.fori_loop(0, n_rows // SUBLANES, body, 0)


def _dispatch(zero_tile, pos_tiles, h2):
    grid_spec = pltpu.PrefetchScalarGridSpec(
        num_scalar_prefetch=1,
        grid=(N_TOK // TMD,),
        in_specs=[pl.BlockSpec((1, 1, 2 * TMD), lambda i, z: (i, 0, 0), memory_space=pltpu.SMEM),
                  pl.BlockSpec((TMD // SUBLANES, SUBLANES, D_MODEL), lambda i, z: (i, 0, 0))],
        out_specs=pl.BlockSpec(memory_space=pl.ANY),
        scratch_shapes=[pltpu.VMEM((TMM, D_MODEL), F32), pltpu.SemaphoreType.DMA((2,))],
    )
    return pl.pallas_call(
        _dispatch_kernel,
        out_shape=jax.ShapeDtypeStruct((P_ROWS, D_MODEL), F32),
        grid_spec=grid_spec,
        compiler_params=_cparams(("arbitrary",)),
        name="moe_dispatch",
    )(zero_tile, pos_tiles, h2.reshape(N_TOK // SUBLANES, SUBLANES, D_MODEL))


def _moe_kernel(te_ref, tile_ref, act_ref, xs_ref, wg_ref, wu_ref, wd_ref, ys_ref,
                wgb_ref, wub_ref, wdb_ref):
    j = pl.program_id(0)
    prev = te_ref[jnp.maximum(j - 1, 0)]

    @pl.when((j == 0) | (te_ref[j] != prev))
    def _():
        wgb_ref[...] = wg_ref[...].astype(BF16)
        wub_ref[...] = wu_ref[...].astype(BF16)
        wdb_ref[...] = wd_ref[...].astype(BF16)

    @pl.when(act_ref[j] == 1)
    def _():
        x = xs_ref[...].astype(BF16)
        hg = _dot(x, wgb_ref[...])
        hu = _dot(x, wub_ref[...])
        a = (hg / (1.0 + jnp.exp(-hg))) * hu
        ys_ref[...] = _dot(a.astype(BF16), wdb_ref[...])

    @pl.when(act_ref[j] == 0)
    def _():
        ys_ref[...] = jnp.zeros_like(ys_ref)


def _moe(layer, te, tile, active, xs, w_gate, w_up, w_down):
    wspec = lambda shape: pl.BlockSpec((None, None) + shape, lambda j, te, tl, act: (layer, te[j], 0, 0))
    grid_spec = pltpu.PrefetchScalarGridSpec(
        num_scalar_prefetch=3,
        grid=(N_MTILES,),
        in_specs=[
            pl.BlockSpec((TMM, D_MODEL), lambda j, te, tl, act: (tl[j], 0)),
            wspec((D_MODEL, D_EXPERT)), wspec((D_MODEL, D_EXPERT)), wspec((D_EXPERT, D_MODEL)),
        ],
        out_specs=pl.BlockSpec((TMM, D_MODEL), lambda j, te, tl, act: (j, 0)),
        scratch_shapes=[pltpu.VMEM((D_MODEL, D_EXPERT), BF16), pltpu.VMEM((D_MODEL, D_EXPERT), BF16),
                        pltpu.VMEM((D_EXPERT, D_MODEL), BF16)],
    )
    return pl.pallas_call(
        _moe_kernel,
        out_shape=jax.ShapeDtypeStruct((P_ROWS, D_MODEL), F32),
        grid_spec=grid_spec,
        compiler_params=_cparams(("arbitrary",)),
        name="moe_grouped_matmul",
    )(te, tile, active, xs, w_gate, w_up, w_down)


def _combine_kernel(pos_ref, ys_hbm, xn_ref, route_ref, gt2_ref, *rest, final):
    o_ref, ybuf_ref, sem = rest[-3:]

    def copy(g, k, src_row):
        return pltpu.make_async_copy(ys_hbm.at[pl.ds(src_row, 1)], ybuf_ref.at[g, pl.ds(k, 1)], sem.at[0])

    _for_each_row(2 * TMD, lambda a, g, k: copy(g, k, pos_ref[0, 0, a]).start(priority=k % 2))
    _for_each_row(2 * TMD, lambda a, g, k: copy(g, k, 0).wait())

    r = route_ref[...]
    yb = ybuf_ref[...].reshape(2 * TMD, D_MODEL)
    y = r[:, 0:1] * yb[:TMD] + r[:, 1:2] * yb[TMD:]
    x = xn_ref[...] + gt2_ref[...] * y
    if final:
        x = _rms(x, rest[0][...])
    o_ref[...] = x


def _combine(layer, xn, ys, pos, route, mod5, final_g, *, sample):
    final = final_g is not None
    if sample:
        n_rows, off = N_SAMPLE, N_PROMPT // TMD
        row_fn = lambda i: 1 + (i * TMD) // DEC_SEQ
    else:
        n_rows, off, row_fn = N_PROMPT, 0, (lambda i: 0)
    tile = lambda w: pl.BlockSpec((TMD, w), lambda i: (i + off, 0))
    in_specs = [pl.BlockSpec((1, 1, 2 * TMD), lambda i: (i + off, 0, 0), memory_space=pltpu.SMEM),
                pl.BlockSpec(memory_space=pl.ANY),
                tile(D_MODEL), tile(LANES), _mod_spec(layer, 5, row_fn)]
    args = [pos, ys, xn, route, mod5]
    if final:
        in_specs.append(pl.BlockSpec((1, D_MODEL), lambda i: (0, 0)))
        args.append(final_g)
    return pl.pallas_call(
        functools.partial(_combine_kernel, final=final),
        out_shape=jax.ShapeDtypeStruct((n_rows, D_MODEL), F32),
        grid=(n_rows // TMD,),
        in_specs=in_specs,
        out_specs=pl.BlockSpec((TMD, D_MODEL), lambda i: (i, 0)),
        scratch_shapes=[pltpu.VMEM((2 * TMD // SUBLANES, SUBLANES, D_MODEL), F32),
                        pltpu.SemaphoreType.DMA((1,))],
        compiler_params=_cparams(("arbitrary",)),
        name="moe_combine_" + ("sample" if sample else "prompt") + ("_final" if final else ""),
    )(*args)


def _pad_segments(w, axis, nseg, seg, segp):
    shape = w.shape
    w = w.reshape(shape[:axis] + (nseg, seg) + shape[axis + 1:])
    pad = [(0, 0)] * w.ndim
    pad[axis + 1] = (0, segp - seg)
    w = jnp.pad(w, pad)
    return w.reshape(shape[:axis] + (nseg * segp,) + shape[axis + 1:])


def _axial_angles(seq, dim):
    rows = seq // GRID_W
    r = jnp.repeat(jnp.arange(rows), GRID_W).astype(F32)
    col = jnp.tile(jnp.arange(GRID_W), rows).astype(F32)
    n = dim // 4
    freqs = ROPE_THETA ** (-jnp.arange(n, dtype=F32) / n)
    return r[:, None] * freqs, col[:, None] * freqs


def _rope_tables(seq, dim, slot, nslots):
    ang_r, ang_c = _axial_angles(seq, dim)
    n = dim // 4
    z = jnp.zeros((seq, n), F32)
    pad = jnp.zeros((seq, slot - dim), F32)
    cr, sr, cc, sc = jnp.cos(ang_r), jnp.sin(ang_r), jnp.cos(ang_c), jnp.sin(ang_c)
    c = jnp.concatenate([cr, cr, cc, cc, pad], axis=1)
    s = jnp.concatenate([sr, sr, sc, sc, pad], axis=1)
    return [jnp.tile(t, (1, nslots)) for t in (c, s)]


def _dft_tables(seq):
    lo_n = 64
    t = jnp.arange(seq, dtype=jnp.int32)
    unit = 2.0 * math.pi / seq

    def cs(mult):
        ang = ((mult[:, None] * t[None, :]) % seq).astype(F32) * unit
        return jnp.cos(ang), jnp.sin(ang)

    ch, sh = cs(jnp.arange(seq // lo_n, dtype=jnp.int32) * lo_n)
    cl, sl = cs(jnp.arange(lo_n, dtype=jnp.int32))
    ch, sh, cl, sl = ch[:, None, :], sh[:, None, :], cl[None, :, :], sl[None, :, :]
    cos = (ch * cl - sh * sl).reshape(seq, seq)
    sin = (sh * cl + ch * sl).reshape(seq, seq)
    return jnp.concatenate([cos, -sin], axis=1).astype(BF16)


def _channel_dft():
    c = jnp.arange(FOURIER_DIM, dtype=jnp.int32)
    ang = ((c[:, None] * c[None, :]) % FOURIER_DIM).astype(F32) * (2.0 * math.pi / FOURIER_DIM)
    eye = jnp.eye(FOURIER_GROUPS, dtype=F32)
    return jnp.concatenate([jnp.kron(eye, jnp.cos(ang)), jnp.kron(eye, jnp.sin(ang))],
                           axis=1).astype(BF16)


def kernel(x_prompt, x_sample, cache_attn_k, cache_attn_v, cache_diff_k, cache_diff_v, c, c_ctx,
           norm1_g, w_mod, b_mod, w_in, w_out, q_norm_g, k_norm_g,
           lambda_q1, lambda_k1, lambda_q2, lambda_k2, subln_g, norm2_g,
           w_grp, b_grp, w_exp, b_exp, w_gate, w_up, w_down, final_g):
    cv = jnp.concatenate([c_ctx[None, :], c, jnp.zeros((MOD_ROWS - 1 - DEC_BATCH, D_MODEL), F32)], axis=0)
    mod5 = _modulation(cv, w_mod, b_mod).reshape(DEPTH, MOD_ROWS, N_MOD, 1, D_MODEL)

    o0, o1, o2, o3, o4 = QA_W + 2 * KA_W, QA_W + 2 * KA_W + 384, QA_W + 2 * KA_W + 768, 1792, 2048
    w_in_b = w_in.astype(BF16)
    w_out_b = w_out.astype(BF16)
    w_in_r = jnp.concatenate([
        w_in_b[..., :o0],
        _pad_segments(w_in_b[..., o0:o1], 2, 2 * DIFF_HEADS, DIFF_QK_DIM, DQK_SLOT),
        _pad_segments(w_in_b[..., o1:o2], 2, 2 * DIFF_HEADS, DIFF_QK_DIM, DQK_SLOT),
        _pad_segments(w_in_b[..., o2:o3], 2, DIFF_HEADS, DIFF_V_DIM, LANES),
        w_in_b[..., o3:o4]], axis=-1)
    woa = jnp.concatenate([w_out_b[:, :QA_W],
                           _pad_segments(w_out_b[:, QA_W:QA_W + DIFF_HEADS * DIFF_V_DIM], 1,
                                         DIFF_HEADS, DIFF_V_DIM, LANES)], axis=1)
    wof = w_out_b[:, QA_W + DIFF_HEADS * DIFF_V_DIM:]
    g1 = norm1_g.reshape(DEPTH, 1, D_MODEL)
    g2 = norm2_g.reshape(DEPTH, 1, D_MODEL)
    gq = jnp.tile(q_norm_g, (1, ATTN_Q_HEADS)).reshape(DEPTH, 1, QA_W)
    gk = jnp.tile(k_norm_g, (1, ATTN_KV_HEADS)).reshape(DEPTH, 1, KA_W)
    subg = _pad_segments(jnp.tile(subln_g, (1, DIFF_HEADS)), 1, DIFF_HEADS, DIFF_V_DIM, LANES
                         ).reshape(DEPTH, 1, VD_W)
    lam_in = jnp.stack([lambda_q1, lambda_k1, lambda_q2, lambda_k2], axis=1)
    w_r = jnp.concatenate([w_grp, w_exp, jnp.zeros((DEPTH, D_MODEL, LANES - N_ROUTE), F32)], axis=-1)
    wr_hi = w_r.astype(BF16)
    wr = jnp.concatenate([wr_hi, (w_r - wr_hi.astype(F32)).astype(BF16)], axis=-1)
    br = jnp.concatenate([b_grp, b_exp, jnp.zeros((DEPTH, LANES - N_ROUTE), F32)], axis=-1
                         ).reshape(DEPTH, 1, LANES)
    seg = np.arange(QA_W) // HEAD_DIM
    bd64 = jnp.asarray((seg[:, None] == seg[None, :]).astype(np.float32) / HEAD_DIM, dtype=BF16)
    ltri = jnp.asarray(np.tril(np.ones((TM, TM), np.float32), -1), dtype=BF16)
    rope_tabs = (_rope_tables(DEC_SEQ, HEAD_DIM, HEAD_DIM, ATTN_Q_HEADS)
                 + _rope_tables(DEC_SEQ, DIFF_QK_DIM, DQK_SLOT, 2 * DIFF_HEADS))
    cs64 = _channel_dft()
    tw_p = _dft_tables(SEQ)
    tw_s = _dft_tables(DEC_SEQ)
    pka = cache_attn_k.reshape(DEC_BATCH, DEPTH, PAST_LEN, KA_W)
    pva = cache_attn_v.reshape(DEC_BATCH, DEPTH, PAST_LEN, KA_W)
    pkd = _pad_segments(cache_diff_k.reshape(DEC_BATCH, DEPTH, PAST_LEN, 2 * DIFF_HEADS * DIFF_QK_DIM),
                        3, 2 * DIFF_HEADS, DIFF_QK_DIM, DQK_SLOT)
    pvd = _pad_segments(cache_diff_v.reshape(DEC_BATCH, DEPTH, PAST_LEN, DIFF_HEADS * DIFF_V_DIM),
                        3, DIFF_HEADS, DIFF_V_DIM, LANES)
    prefix = [a.astype(BF16) for a in (pka, pva, pkd, pvd)]

    caches = []
    x_p = x_prompt.reshape(N_PROMPT, D_MODEL)
    x_s = x_sample.reshape(N_SAMPLE, D_MODEL)
    for layer in range(DEPTH):
        pre_p = _pre_attention(layer, x_p, mod5, g1, w_in_r, gq, gk, bd64, None, sample=False)
        pre_s = _pre_attention(layer, x_s, mod5, g1, w_in_r, gq, gk, bd64, rope_tabs, sample=True)
        caches.append(pre_p[8:])
        att_p = _attention(layer, pre_p, None, lam_in, subg, sample=False)
        att_s = _attention(layer, pre_s, prefix, lam_in, subg, sample=True)
        fo_p = _fourier(pre_p[7], cs64, tw_p, sample=False)
        fo_s = _fourier(pre_s[7], cs64, tw_s, sample=True)
        xn, h2, route, counts, route_t = _post_attention(layer, (x_p, x_s), (att_p, att_s), (fo_p, fo_s),
                                                woa, wof, mod5, g2, wr, br, ltri)
        pos, te, tile, active, zero_tile = _sort_plan(route_t, counts)
        xs = _dispatch(zero_tile, pos, h2)
        ys = _moe(layer, te, tile, active, xs, w_gate, w_up, w_down)
        fg = final_g.reshape(1, D_MODEL) if layer + 1 == DEPTH else None
        x_p = _combine(layer, xn, ys, pos, route, mod5, fg, sample=False)
        x_s = _combine(layer, xn, ys, pos, route, mod5, fg, sample=True)
    y_prompt, y_sample = x_p, x_s

    def stack(i, shape, keep):
        arrs = [cl[i].reshape(shape)[..., :keep] for cl in caches]
        return jnp.stack(arrs, axis=1)

    new_attn_k = stack(0, (BATCH, SEQ, ATTN_KV_HEADS, HEAD_DIM), HEAD_DIM)
    new_attn_v = stack(1, (BATCH, SEQ, ATTN_KV_HEADS, HEAD_DIM), HEAD_DIM)
    new_diff_k = stack(2, (BATCH, SEQ, DIFF_HEADS, 2, DQK_SLOT), DIFF_QK_DIM)
    new_diff_v = stack(3, (BATCH, SEQ, DIFF_HEADS, LANES), DIFF_V_DIM)
    return (y_prompt.reshape(BATCH, SEQ, D_MODEL), y_sample.reshape(DEC_BATCH, DEC_SEQ, D_MODEL),
            new_attn_k, new_attn_v, new_diff_k, new_diff_v)
```
